```python
import math
import jax, jax.numpy as jnp
from jax import lax
import numpy as np

D_MODEL = 2048
BATCH = 1
SEQ = 8192
DEPTH = 4

CHUNK = 64
Q_BLOCK = 128
D_CONV = 1024
CONV_WIDTH = 3
N_HEADS = 8
HEAD_DIM = 64
V_DIM = 2 * HEAD_DIM
D_ATTN = N_HEADS * V_DIM
ROPE_THETA = 10000.0
EPS = 1e-6
LAMBDA_STD = 0.1
D_IN = 4 * D_CONV + 4 * D_ATTN + 2 * D_MODEL

kernel_name = "hybrid_shortconv_diffattn_block"


def rms_norm(x, g):
    xf = x.astype(jnp.float32)
    y = xf * lax.rsqrt(jnp.mean(xf * xf, axis=-1, keepdims=True) + EPS)
    return (y * g.astype(jnp.float32)).astype(x.dtype)


def split_columns(p, sizes):
    outs, start = [], 0
    for s in sizes:
        outs.append(p[..., start:start + s])
        start += s
    return outs


def rope(x, pos):
    half = HEAD_DIM // 2
    inv = ROPE_THETA ** (-jnp.arange(half, dtype=jnp.float32) / half)
    ang = pos.astype(jnp.float32)[:, None] * inv[None, :]
    cos = jnp.cos(ang)[None, :, None, None, :]
    sin = jnp.sin(ang)[None, :, None, None, :]
    xf = x.astype(jnp.float32)
    x1, x2 = xf[..., :half], xf[..., half:]
    out = jnp.concatenate([x1 * cos - x2 * sin, x2 * cos + x1 * sin], axis=-1)
    return out.astype(x.dtype)


def short_conv_branch(u, b_gate, c_gate, z, conv_w, w_out):
    v = c_gate * u
    y = lax.conv_general_dilated(
        v, conv_w[:, None, :], window_strides=(1,),
        padding=((CONV_WIDTH - 1, 0),),
        dimension_numbers=("NWC", "WIO", "NWC"),
        feature_group_count=D_CONV)
    return (b_gate * y * jax.nn.silu(z)) @ w_out


def diff_attention(q, k, v, lam, lambda_init, subln_g):
    bsz, seq = q.shape[0], q.shape[1]
    scale = HEAD_DIM ** -0.5
    k_chunk = jnp.arange(seq) // CHUNK

    def block(i):
        start = i * Q_BLOCK
        qb = lax.dynamic_slice_in_dim(q, start, Q_BLOCK, axis=1)
        s = jnp.einsum("bqhcd,bkhcd->bhcqk", qb, k,
                       preferred_element_type=jnp.float32) * scale
        q_chunk = (start + jnp.arange(Q_BLOCK)) // CHUNK
        mask = k_chunk[None, :] <= q_chunk[:, None]
        s = jnp.where(mask, s, -jnp.inf)
        p = jax.nn.softmax(s, axis=-1)
        a = p[:, :, 0] - lam * p[:, :, 1]
        return jnp.einsum("bhqk,bkhe->bqhe", a.astype(v.dtype), v)

    out = lax.map(block, jnp.arange(seq // Q_BLOCK))
    out = jnp.moveaxis(out, 0, 1).reshape(bsz, seq, N_HEADS, V_DIM)
    out = rms_norm(out, subln_g) * (1.0 - lambda_init)
    return out.reshape(bsz, seq, D_ATTN)


def setup_inputs(seed: int = 0) -> dict:
    key = jax.random.key(seed)
    ks = jax.random.split(key, 18)
    f32 = jnp.float32
    n = lambda k, shape, s: jax.random.normal(k, shape, f32) * s
    return {
        "x": n(ks[0], (BATCH, SEQ, D_MODEL), 1.0),
        "c": n(ks[1], (BATCH, D_MODEL), 1.0),
        "ada_w": n(ks[2], (DEPTH, D_MODEL, 3 * D_MODEL), D_MODEL ** -0.5),
        "ada_b": n(ks[3], (DEPTH, 3 * D_MODEL), 0.01),
        "norm_g": 1.0 + n(ks[4], (DEPTH, D_MODEL), 0.01),
        "w_in": n(ks[5], (DEPTH, D_MODEL, D_IN), D_MODEL ** -0.5),
        "conv_w": n(ks[6], (DEPTH, CONV_WIDTH, D_CONV), CONV_WIDTH ** -0.5),
        "w_conv_out": n(ks[7], (DEPTH, D_CONV, D_MODEL), D_CONV ** -0.5),
        "q_norm_g": 1.0 + n(ks[8], (DEPTH, HEAD_DIM), 0.01),
        "k_norm_g": 1.0 + n(ks[9], (DEPTH, HEAD_DIM), 0.01),
        "lam_q1": n(ks[10], (DEPTH, HEAD_DIM), LAMBDA_STD),
        "lam_k1": n(ks[11], (DEPTH, HEAD_DIM), LAMBDA_STD),
        "lam_q2": n(ks[12], (DEPTH, HEAD_DIM), LAMBDA_STD),
        "lam_k2": n(ks[13], (DEPTH, HEAD_DIM), LAMBDA_STD),
        "subln_g": 1.0 + n(ks[14], (DEPTH, V_DIM), 0.01),
        "w_attn_out": n(ks[15], (DEPTH, D_ATTN, D_MODEL), D_ATTN ** -0.5),
        "w_o": n(ks[16], (DEPTH, D_MODEL, D_MODEL), D_MODEL ** -0.5),
    }


def reference(x, c, ada_w, ada_b, norm_g, w_in, conv_w, w_conv_out, q_norm_g, k_norm_g,
              lam_q1, lam_k1, lam_q2, lam_k2, subln_g, w_attn_out, w_o):
    bsz, seq = x.shape[0], x.shape[1]
    pos = jnp.arange(seq)
    c_act = jax.nn.silu(c)
    sizes = [D_CONV] * 4 + [D_ATTN] * 4 + [D_MODEL] * 2
    for l in range(DEPTH):
        mod = c_act @ ada_w[l] + ada_b[l]
        shift, scale, gate = jnp.split(mod, 3, axis=-1)
        h = rms_norm(x, norm_g[l]) * (1.0 + scale[:, None, :]) + shift[:, None, :]

        proj = h @ w_in[l]
        u, bg, cg, za, q, k, v, zb, ga, gb = split_columns(proj, sizes)

        y_conv = short_conv_branch(u, bg, cg, za, conv_w[l], w_conv_out[l])

        q = rope(rms_norm(q.reshape(bsz, seq, N_HEADS, 2, HEAD_DIM), q_norm_g[l]), pos)
        k = rope(rms_norm(k.reshape(bsz, seq, N_HEADS, 2, HEAD_DIM), k_norm_g[l]), pos)
        v = v.reshape(bsz, seq, N_HEADS, V_DIM)
        lambda_init = 0.8 - 0.6 * math.exp(-0.3 * l)
        lam = (jnp.exp(jnp.sum(lam_q1[l].astype(jnp.float32) * lam_k1[l].astype(jnp.float32)))
               - jnp.exp(jnp.sum(lam_q2[l].astype(jnp.float32) * lam_k2[l].astype(jnp.float32)))
               + lambda_init)
        o_attn = diff_attention(q, k, v, lam, lambda_init, subln_g[l])
        y_attn = (o_attn * jax.nn.silu(zb)) @ w_attn_out[l]

        merged = jax.nn.sigmoid(ga) * y_conv + jax.nn.sigmoid(gb) * y_attn
        x = x + gate[:, None, :] * (merged @ w_o[l])
    return x
```

```python
import functools
import math

import jax
import jax.numpy as jnp
from jax import lax
from jax.experimental import pallas as pl
from jax.experimental.pallas import tpu as pltpu

F32 = jnp.float32
BF16 = jnp.bfloat16

D_MODEL = 2048
D_CONV = 1024
N_HEADS = 8
HEAD_DIM = 64
V_DIM = 2 * HEAD_DIM
D_ATTN = N_HEADS * V_DIM
CHUNK = 64
ROPE_THETA = 10000.0
EPS = 1e-6
OFF_U, OFF_BG, OFF_CG, OFF_ZA = 0, D_CONV, 2 * D_CONV, 3 * D_CONV
OFF_Q = 4 * D_CONV
OFF_K = OFF_Q + D_ATTN
OFF_V = OFF_K + D_ATTN
OFF_ZB = OFF_V + D_ATTN
OFF_G = OFF_ZB + D_ATTN

LANES = 128
BF16_SUBLANES = 16
MIB = 1024 * 1024

ROW_TILE = 512
COL_TILE = 256
Q_TILE = 256
KV_TILE = 256
OUT_COL_TILE = 512
MOD_COL_TILE = 768


def _params(semantics, vmem_mib):
    return pltpu.CompilerParams(dimension_semantics=semantics,
                                vmem_limit_bytes=vmem_mib * MIB)


def _dot(a, b):
    return jnp.dot(a, b, preferred_element_type=F32)


def _mod_kernel(c_ref, w_ref, b_ref, o_ref):
    c = c_ref[...]
    o_ref[0] = _dot(c * jax.nn.sigmoid(c), w_ref[0]) + b_ref[0]


def _modulation(c, ada_w, ada_b):
    depth, d, n = ada_w.shape
    c8 = jnp.broadcast_to(c, (8, d))
    out = pl.pallas_call(
        _mod_kernel,
        grid=(depth, n // MOD_COL_TILE),
        in_specs=[
            pl.BlockSpec((8, d), lambda l, j: (0, 0)),
            pl.BlockSpec((1, d, MOD_COL_TILE), lambda l, j: (l, 0, j)),
            pl.BlockSpec((1, 1, MOD_COL_TILE), lambda l, j: (l, 0, j)),
        ],
        out_specs=pl.BlockSpec((1, 8, MOD_COL_TILE), lambda l, j: (l, 0, j)),
        out_shape=jax.ShapeDtypeStruct((depth, 8, n), F32),
        compiler_params=_params(("parallel", "parallel"), 32),
        name="adaln_mod",
    )(c8, ada_w, ada_b.reshape(depth, 1, n))
    return out[:, 0:1, :]


def _norm_kernel(x_ref, g_ref, scale_ref, shift_ref, h_ref):
    x = x_ref[...]
    y = x * lax.rsqrt(jnp.mean(x * x, axis=-1, keepdims=True) + EPS) * g_ref[...]
    h_ref[...] = (y * (1.0 + scale_ref[...]) + shift_ref[...]).astype(BF16)


def _norm_mod(x, g, scale, shift):
    s, d = x.shape
    row = pl.BlockSpec((1, d), lambda m: (0, 0))
    return pl.pallas_call(
        _norm_kernel,
        grid=(s // ROW_TILE,),
        in_specs=[pl.BlockSpec((ROW_TILE, d), lambda m: (m, 0)), row, row, row],
        out_specs=pl.BlockSpec((ROW_TILE, d), lambda m: (m, 0)),
        out_shape=jax.ShapeDtypeStruct((s, d), BF16),
        compiler_params=_params(("parallel",), 32),
        name="norm_mod",
    )(x, g, scale, shift)


def _conv_kernel(h_ref, halo_ref, wu_ref, wb_ref, wc_ref, wz_ref, cw_ref, o_ref):
    h = h_ref[...]
    wu = wu_ref[...]
    wc = wc_ref[...]
    v = _dot(h, wc) * _dot(h, wu)
    halo = halo_ref[...]
    vh = _dot(halo, wc) * _dot(halo, wu)
    vh = jnp.where(pl.program_id(0) > 0, vh, 0.0)
    prev1 = vh[BF16_SUBLANES - 1:BF16_SUBLANES, :]
    prev2 = vh[BF16_SUBLANES - 2:BF16_SUBLANES - 1, :]
    row = lax.broadcasted_iota(jnp.int32, v.shape, 0)
    v1 = jnp.where(row == 0, prev1, pltpu.roll(v, 1, axis=0))
    v2 = jnp.where(row == 0, prev2,
                   jnp.where(row == 1, prev1, pltpu.roll(v, 2, axis=0)))
    cw = cw_ref[...]
    y = cw[0:1, :] * v2 + cw[1:2, :] * v1 + cw[2:3, :] * v
    za = _dot(h, wz_ref[...])
    o_ref[...] = (_dot(h, wb_ref[...]) * y * (za * jax.nn.sigmoid(za))).astype(BF16)


def _conv_branch(h, w_in, conv_w):
    s, d = h.shape
    tm, tn = ROW_TILE, COL_TILE

    def wspec(off):
        return pl.BlockSpec((d, tn), lambda m, n, o=off // tn: (0, o + n))

    return pl.pallas_call(
        _conv_kernel,
        grid=(s // tm, D_CONV // tn),
        in_specs=[
            pl.BlockSpec((tm, d), lambda m, n: (m, 0)),
            pl.BlockSpec((BF16_SUBLANES, d),
                         lambda m, n: (jnp.maximum(m * (tm // BF16_SUBLANES) - 1, 0), 0)),
            wspec(OFF_U), wspec(OFF_BG), wspec(OFF_CG), wspec(OFF_ZA),
            pl.BlockSpec((3, tn), lambda m, n: (0, n)),
        ],
        out_specs=pl.BlockSpec((tm, tn), lambda m, n: (m, n)),
        out_shape=jax.ShapeDtypeStruct((s, D_CONV), BF16),
        compiler_params=_params(("parallel", "parallel"), 48),
        name="conv_branch",
    )(h, h, w_in, w_in, w_in, w_in, conv_w)


def _group_norm_rope(r, gsum, g, cos, sin):
    rr = r * r
    hi = rr.astype(BF16)
    lo = (rr - hi.astype(F32)).astype(BF16)
    ss = _dot(hi, gsum) + _dot(lo, gsum)
    rn = r * lax.rsqrt(ss * (1.0 / HEAD_DIM) + EPS) * g
    half = HEAD_DIM // 2
    lane = lax.broadcasted_iota(jnp.int32, (r.shape[0], LANES), 1)
    first_half = (lane % HEAD_DIM) < half
    parts = []
    for a in range(r.shape[1] // LANES):
        x = rn[:, a * LANES:(a + 1) * LANES]
        partner = jnp.where(first_half, pltpu.roll(x, LANES - half, axis=1),
                            pltpu.roll(x, half, axis=1))
        parts.append(x * cos + partner * sin)
    return jnp.concatenate(parts, axis=1)


def _qk_kernel(h_ref, wq_ref, wk_ref, gsum_ref, gq_ref, gk_ref, cos_ref, sin_ref,
               qt_ref, k_ref):
    h = h_ref[...]
    gsum = gsum_ref[...]
    cos = cos_ref[...]
    sin = sin_ref[...]
    q = _group_norm_rope(_dot(h, wq_ref[...]), gsum, gq_ref[...], cos, sin)
    qt_ref[...] = (q * (HEAD_DIM ** -0.5)).T.astype(BF16)
    k = _group_norm_rope(_dot(h, wk_ref[...]), gsum, gk_ref[...], cos, sin)
    k_ref[...] = k.astype(BF16)


def _rope_tables(s):
    half = HEAD_DIM // 2
    inv = ROPE_THETA ** (-jnp.arange(half, dtype=F32) / half)
    ang = jnp.arange(s).astype(F32)[:, None] * inv[None, :]
    reps = LANES // half
    cos = jnp.tile(jnp.cos(ang), (1, reps))
    sign = jnp.where((jnp.arange(LANES) % HEAD_DIM) < half, -1.0, 1.0).astype(F32)
    sin = jnp.tile(jnp.sin(ang), (1, reps)) * sign[None, :]
    return cos, sin


def _qk_branch(h, w_in, gq, gk, cos, sin):
    s, d = h.shape
    tm, tn = ROW_TILE, COL_TILE
    grp = jnp.arange(tn) // HEAD_DIM
    gsum = (grp[:, None] == grp[None, :]).astype(BF16)
    gq_t = jnp.tile(gq, tn // HEAD_DIM)[None, :]
    gk_t = jnp.tile(gk, tn // HEAD_DIM)[None, :]

    def wspec(off):
        return pl.BlockSpec((d, tn), lambda m, n, o=off // tn: (0, o + n))

    const = lambda shape: pl.BlockSpec(shape, lambda m, n: (0, 0))
    return pl.pallas_call(
        _qk_kernel,
        grid=(s // tm, D_ATTN // tn),
        in_specs=[
            pl.BlockSpec((tm, d), lambda m, n: (m, 0)),
            wspec(OFF_Q), wspec(OFF_K),
            const((tn, tn)), const((1, tn)), const((1, tn)),
            pl.BlockSpec((tm, LANES), lambda m, n: (m, 0)),
            pl.BlockSpec((tm, LANES), lambda m, n: (m, 0)),
        ],
        out_specs=[
            pl.BlockSpec((tn, tm), lambda m, n: (n, m)),
            pl.BlockSpec((tm, tn), lambda m, n: (m, n)),
        ],
        out_shape=[
            jax.ShapeDtypeStruct((D_ATTN, s), BF16),
            jax.ShapeDtypeStruct((s, D_ATTN), BF16),
        ],
        compiler_params=_params(("parallel", "parallel"), 48),
        name="qk_branch",
    )(h, w_in, w_in, gsum, gq_t, gk_t, cos, sin)


def _vt_kernel(h_ref, w_ref, o_ref):
    vt = _dot(h_ref[...], w_ref[...]).astype(BF16).T
    for hh in range(o_ref.shape[0]):
        for t in range(o_ref.shape[1]):
            o_ref[hh, t] = vt[hh * V_DIM:(hh + 1) * V_DIM, t * KV_TILE:(t + 1) * KV_TILE]


def _v_branch(h, w_in):
    s, d = h.shape
    tm, tn = ROW_TILE, COL_TILE
    hpt = tn // V_DIM
    return pl.pallas_call(
        _vt_kernel,
        grid=(s // tm, D_ATTN // tn),
        in_specs=[
            pl.BlockSpec((tm, d), lambda m, n: (m, 0)),
            pl.BlockSpec((d, tn), lambda m, n, o=OFF_V // tn: (0, o + n)),
        ],
        out_specs=pl.BlockSpec((hpt, tm // KV_TILE, V_DIM, KV_TILE),
                               lambda m, n: (n, m, 0, 0)),
        out_shape=jax.ShapeDtypeStruct((N_HEADS, s // KV_TILE, V_DIM, KV_TILE), BF16),
        compiler_params=_params(("parallel", "parallel"), 48),
        name="v_branch",
    )(h, w_in)


def _act_kernel(h_ref, w_ref, o_ref, *, act):
    r = _dot(h_ref[...], w_ref[...])
    sg = jax.nn.sigmoid(r)
    o_ref[...] = (r * sg if act == "silu" else sg).astype(BF16)


def _act_branch(h, w_in, off, width, act):
    s, d = h.shape
    tm, tn = ROW_TILE, 2 * COL_TILE
    return pl.pallas_call(
        functools.partial(_act_kernel, act=act),
        grid=(s // tm, width // tn),
        in_specs=[
            pl.BlockSpec((tm, d), lambda m, n: (m, 0)),
            pl.BlockSpec((d, tn), lambda m, n, o=off // tn: (0, o + n)),
        ],
        out_specs=pl.BlockSpec((tm, tn), lambda m, n: (m, n)),
        out_shape=jax.ShapeDtypeStruct((s, width), BF16),
        compiler_params=_params(("parallel", "parallel"), 48),
        name="proj_" + act,
    )(h, w_in)


def _attn_kernel(lamv_ref, qt_ref, k_ref, vt_ref, zs_ref, g_ref, o_ref, *, lambda_init):
    tq = qt_ref.shape[1]
    tk = vt_ref.shape[2]
    i = pl.program_id(1)

    qt = qt_ref[...]
    comp = lax.broadcasted_iota(jnp.int32, qt.shape, 0) < HEAD_DIM
    zero = jnp.zeros_like(qt)
    qz = jnp.concatenate([jnp.where(comp, qt, zero), jnp.where(comp, zero, qt)], axis=1)
    ones = jnp.ones((BF16_SUBLANES, tk), BF16)

    def step(j, carry, mask):
        m, acc = carry
        kj = k_ref[pl.ds(pl.multiple_of(j * tk, tk), tk), :]
        s = _dot(kj, qz)
        if mask is not None:
            s = jnp.where(mask, s, -jnp.inf)
        m_new = jnp.maximum(m, jnp.max(s, axis=0, keepdims=True))
        alpha = jnp.exp(m - m_new)
        p = jnp.exp(s - m_new).astype(BF16)
        vj = jnp.concatenate([vt_ref[j], ones], axis=0)
        return m_new, acc * alpha + _dot(vj, p)

    m0 = jnp.full((1, 2 * tq), -jnp.inf, F32)
    acc0 = jnp.zeros((V_DIM + BF16_SUBLANES, 2 * tq), F32)
    carry = lax.fori_loop(0, i, lambda j, c: step(j, c, None), (m0, acc0))
    kc = lax.broadcasted_iota(jnp.int32, (tk, 2 * tq), 0) // CHUNK
    qc = (lax.broadcasted_iota(jnp.int32, (tk, 2 * tq), 1) % tq) // CHUNK
    _, acc = step(i, carry, kc <= qc)

    o = acc[:V_DIM, :] / acc[V_DIM:V_DIM + 1, :]
    lamv = lamv_ref[...]
    lam = (jnp.exp(jnp.sum(lamv[0:1] * lamv[1:2], axis=-1, keepdims=True))
           - jnp.exp(jnp.sum(lamv[2:3] * lamv[3:4], axis=-1, keepdims=True))
           + lambda_init)
    d = (o[:, :tq] - lam * o[:, tq:]).T
    dn = d * lax.rsqrt(jnp.mean(d * d, axis=-1, keepdims=True) + EPS) * g_ref[...]
    o_ref[...] = (dn * (1.0 - lambda_init) * zs_ref[...].astype(F32)).astype(BF16)


def _attention(lamv, qt, k, vt, zs, subln_g, lambda_init):
    s = k.shape[0]
    assert Q_TILE == KV_TILE and Q_TILE % CHUNK == 0
    return pl.pallas_call(
        functools.partial(_attn_kernel, lambda_init=lambda_init),
        grid=(N_HEADS, s // Q_TILE),
        in_specs=[
            pl.BlockSpec((4, HEAD_DIM), lambda h, i: (0, 0)),
            pl.BlockSpec((V_DIM, Q_TILE), lambda h, i: (h, i)),
            pl.BlockSpec((s, V_DIM), lambda h, i: (0, h)),
            pl.BlockSpec((None, s // KV_TILE, V_DIM, KV_TILE), lambda h, i: (h, 0, 0, 0)),
            pl.BlockSpec((Q_TILE, V_DIM), lambda h, i: (i, h)),
            pl.BlockSpec((1, V_DIM), lambda h, i: (0, 0)),
        ],
        out_specs=pl.BlockSpec((Q_TILE, V_DIM), lambda h, i: (i, h)),
        out_shape=jax.ShapeDtypeStruct((s, D_ATTN), BF16),
        compiler_params=_params(("parallel", "parallel"), 48),
        name="diff_attn",
    )(lamv, qt, k, vt, zs, subln_g)


def _out_kernel(yc_ref, oz_ref, ga_ref, gb_ref, wc_ref, wa_ref, wo_ref, x_ref, gate_ref,
                o_ref, merged_ref):
    @pl.when(pl.program_id(1) == 0)
    def _():
        a = _dot(yc_ref[...], wc_ref[...])
        b = _dot(oz_ref[...], wa_ref[...])
        merged_ref[...] = (ga_ref[...].astype(F32) * a
                           + gb_ref[...].astype(F32) * b).astype(BF16)

    o_ref[...] = x_ref[...] + gate_ref[...] * _dot(merged_ref[...], wo_ref[...])


def _merge_out(yc, oz, gates, wc, wa, wo, x, gate):
    s, d = x.shape
    tm, tn = ROW_TILE, OUT_COL_TILE
    return pl.pallas_call(
        _out_kernel,
        grid=(s // tm, d // tn),
        in_specs=[
            pl.BlockSpec((tm, D_CONV), lambda m, n: (m, 0)),
            pl.BlockSpec((tm, D_ATTN), lambda m, n: (m, 0)),
            pl.BlockSpec((tm, d), lambda m, n: (m, 0)),
            pl.BlockSpec((tm, d), lambda m, n: (m, 1)),
            pl.BlockSpec((D_CONV, d), lambda m, n: (0, 0)),
            pl.BlockSpec((D_ATTN, d), lambda m, n: (0, 0)),
            pl.BlockSpec((d, tn), lambda m, n: (0, n)),
            pl.BlockSpec((tm, tn), lambda m, n: (m, n)),
            pl.BlockSpec((1, tn), lambda m, n: (0, n)),
        ],
        out_specs=pl.BlockSpec((tm, tn), lambda m, n: (m, n)),
        out_shape=jax.ShapeDtypeStruct((s, d), F32),
        scratch_shapes=[pltpu.VMEM((tm, d), BF16)],
        compiler_params=_params(("parallel", "arbitrary"), 56),
        name="merge_out",
    )(yc, oz, gates, gates, wc, wa, wo, x, gate)


def kernel(x, c, ada_w, ada_b, norm_g, w_in, conv_w, w_conv_out, q_norm_g, k_norm_g,
           lam_q1, lam_k1, lam_q2, lam_k2, subln_g, w_attn_out, w_o):
    bsz, seq, d = x.shape
    depth = w_in.shape[0]
    assert bsz == 1 and d == D_MODEL and seq % ROW_TILE == 0 and seq % Q_TILE == 0
    xs = x[0]
    mod = _modulation(c, ada_w, ada_b)
    cos, sin = _rope_tables(seq)
    lamv = jnp.stack([lam_q1, lam_k1, lam_q2, lam_k2], axis=1).astype(F32)
    for l in range(depth):
        shift, scale, gate = (mod[l, :, i * d:(i + 1) * d] for i in range(3))
        w_l = w_in[l].astype(BF16)
        h = _norm_mod(xs, norm_g[l][None, :], scale, shift)
        yc = _conv_branch(h, w_l, conv_w[l])
        qt, k = _qk_branch(h, w_l, q_norm_g[l], k_norm_g[l], cos, sin)
        vt = _v_branch(h, w_l)
        zs = _act_branch(h, w_l, OFF_ZB, D_ATTN, "silu")
        gates = _act_branch(h, w_l, OFF_G, 2 * d, "sigmoid")
        lambda_init = 0.8 - 0.6 * math.exp(-0.3 * l)
        oz = _attention(lamv[l], qt, k, vt, zs, subln_g[l][None, :], lambda_init)
        xs = _merge_out(yc, oz, gates, w_conv_out[l].astype(BF16),
                        w_attn_out[l].astype(BF16), w_o[l].astype(BF16), xs, gate)
    return xs[None]
```

```python
import functools
import math

import jax
import jax.numpy as jnp
from jax import lax
from jax.experimental import pallas as pl
from jax.experimental.pallas import tpu as pltpu

F32 = jnp.float32
BF16 = jnp.bfloat16

D_MODEL = 2048
D_CONV = 1024
N_HEADS = 8
HEAD_DIM = 64
V_DIM = 2 * HEAD_DIM
D_ATTN = N_HEADS * V_DIM
CHUNK = 64
ROPE_THETA = 10000.0
EPS = 1e-6
LOG2_E = math.log2(math.e)
OFF_U, OFF_BG, OFF_CG, OFF_ZA = 0, D_CONV, 2 * D_CONV, 3 * D_CONV
OFF_Q = 4 * D_CONV
OFF_K = OFF_Q + D_ATTN
OFF_V = OFF_K + D_ATTN
OFF_ZB = OFF_V + D_ATTN
OFF_G = OFF_ZB + D_ATTN

LANES = 128
BF16_SUBLANES = 16
MIB = 1024 * 1024

ROW_TILE = 512
COL_TILE = 256
Q_TILE = 512
KV_TILE = 512
OUT_COL_TILE = 512
MOD_COL_TILE = 768


def _params(semantics, vmem_mib):
    return pltpu.CompilerParams(dimension_semantics=semantics,
                                vmem_limit_bytes=vmem_mib * MIB)


def _dot(a, b):
    return jnp.dot(a, b, preferred_element_type=F32)


def _mod_kernel(c_ref, w_ref, b_ref, o_ref):
    c = c_ref[...]
    o_ref[0] = _dot(c * jax.nn.sigmoid(c), w_ref[0]) + b_ref[0]


def _modulation(c, ada_w, ada_b):
    depth, d, n = ada_w.shape
    c8 = jnp.broadcast_to(c, (8, d))
    out = pl.pallas_call(
        _mod_kernel,
        grid=(depth, n // MOD_COL_TILE),
        in_specs=[
            pl.BlockSpec((8, d), lambda l, j: (0, 0)),
            pl.BlockSpec((1, d, MOD_COL_TILE), lambda l, j: (l, 0, j)),
            pl.BlockSpec((1, 1, MOD_COL_TILE), lambda l, j: (l, 0, j)),
        ],
        out_specs=pl.BlockSpec((1, 8, MOD_COL_TILE), lambda l, j: (l, 0, j)),
        out_shape=jax.ShapeDtypeStruct((depth, 8, n), F32),
        compiler_params=_params(("parallel", "parallel"), 32),
        name="adaln_mod",
    )(c8, ada_w, ada_b.reshape(depth, 1, n))
    return out[:, 0:1, :]


def _norm_kernel(x_ref, g_ref, scale_ref, shift_ref, h_ref):
    x = x_ref[...]
    y = x * lax.rsqrt(jnp.mean(x * x, axis=-1, keepdims=True) + EPS) * g_ref[...]
    h_ref[...] = (y * (1.0 + scale_ref[...]) + shift_ref[...]).astype(BF16)


def _norm_mod(x, g, scale, shift):
    s, d = x.shape
    row = pl.BlockSpec((1, d), lambda m: (0, 0))
    return pl.pallas_call(
        _norm_kernel,
        grid=(s // ROW_TILE,),
        in_specs=[pl.BlockSpec((ROW_TILE, d), lambda m: (m, 0)), row, row, row],
        out_specs=pl.BlockSpec((ROW_TILE, d), lambda m: (m, 0)),
        out_shape=jax.ShapeDtypeStruct((s, d), BF16),
        compiler_params=_params(("parallel",), 32),
        name="norm_mod",
    )(x, g, scale, shift)


def _conv_kernel(h_ref, halo_ref, wu_ref, wb_ref, wc_ref, wz_ref, cw_ref, o_ref):
    h = h_ref[...]
    wu = wu_ref[...]
    wc = wc_ref[...]
    v = _dot(h, wc) * _dot(h, wu)
    halo = halo_ref[...]
    vh = _dot(halo, wc) * _dot(halo, wu)
    vh = jnp.where(pl.program_id(0) > 0, vh, 0.0)
    prev1 = vh[BF16_SUBLANES - 1:BF16_SUBLANES, :]
    prev2 = vh[BF16_SUBLANES - 2:BF16_SUBLANES - 1, :]
    row = lax.broadcasted_iota(jnp.int32, v.shape, 0)
    v1 = jnp.where(row == 0, prev1, pltpu.roll(v, 1, axis=0))
    v2 = jnp.where(row == 0, prev2,
                   jnp.where(row == 1, prev1, pltpu.roll(v, 2, axis=0)))
    cw = cw_ref[...]
    y = cw[0:1, :] * v2 + cw[1:2, :] * v1 + cw[2:3, :] * v
    za = _dot(h, wz_ref[...])
    o_ref[...] = (_dot(h, wb_ref[...]) * y * (za * jax.nn.sigmoid(za))).astype(BF16)


def _conv_branch(h, w_in, conv_w):
    s, d = h.shape
    tm, tn = ROW_TILE, COL_TILE

    def wspec(off):
        return pl.BlockSpec((d, tn), lambda m, n, o=off // tn: (0, o + n))

    return pl.pallas_call(
        _conv_kernel,
        grid=(s // tm, D_CONV // tn),
        in_specs=[
            pl.BlockSpec((tm, d), lambda m, n: (m, 0)),
            pl.BlockSpec((BF16_SUBLANES, d),
                         lambda m, n: (jnp.maximum(m * (tm // BF16_SUBLANES) - 1, 0), 0)),
            wspec(OFF_U), wspec(OFF_BG), wspec(OFF_CG), wspec(OFF_ZA),
            pl.BlockSpec((3, tn), lambda m, n: (0, n)),
        ],
        out_specs=pl.BlockSpec((tm, tn), lambda m, n: (m, n)),
        out_shape=jax.ShapeDtypeStruct((s, D_CONV), BF16),
        compiler_params=_params(("parallel", "parallel"), 48),
        name="conv_branch",
    )(h, h, w_in, w_in, w_in, w_in, conv_w)


def _group_norm_rope(r, gsum, g, cos, sin):
    rr = r * r
    hi = rr.astype(BF16)
    lo = (rr - hi.astype(F32)).astype(BF16)
    ss = _dot(hi, gsum) + _dot(lo, gsum)
    rn = r * lax.rsqrt(ss * (1.0 / HEAD_DIM) + EPS) * g
    half = HEAD_DIM // 2
    lane = lax.broadcasted_iota(jnp.int32, (r.shape[0], LANES), 1)
    first_half = (lane % HEAD_DIM) < half
    parts = []
    for a in range(r.shape[1] // LANES):
        x = rn[:, a * LANES:(a + 1) * LANES]
        partner = jnp.where(first_half, pltpu.roll(x, LANES - half, axis=1),
                            pltpu.roll(x, half, axis=1))
        parts.append(x * cos + partner * sin)
    return jnp.concatenate(parts, axis=1)


def _qk_kernel(h_ref, wq_ref, wk_ref, gsum_ref, gq_ref, gk_ref, cos_ref, sin_ref,
               qt_ref, k_ref):
    h = h_ref[...]
    gsum = gsum_ref[...]
    cos = cos_ref[...]
    sin = sin_ref[...]
    q = _group_norm_rope(_dot(h, wq_ref[...]), gsum, gq_ref[...], cos, sin)
    qt_ref[...] = (q * (HEAD_DIM ** -0.5 * LOG2_E)).T.astype(BF16)
    k = _group_norm_rope(_dot(h, wk_ref[...]), gsum, gk_ref[...], cos, sin)
    k_ref[...] = k.astype(BF16)


def _rope_tables(s):
    half = HEAD_DIM // 2
    inv = ROPE_THETA ** (-jnp.arange(half, dtype=F32) / half)
    ang = jnp.arange(s).astype(F32)[:, None] * inv[None, :]
    reps = LANES // half
    cos = jnp.tile(jnp.cos(ang), (1, reps))
    sign = jnp.where((jnp.arange(LANES) % HEAD_DIM) < half, -1.0, 1.0).astype(F32)
    sin = jnp.tile(jnp.sin(ang), (1, reps)) * sign[None, :]
    return cos, sin


def _qk_branch(h, w_in, gq, gk, cos, sin):
    s, d = h.shape
    tm, tn = ROW_TILE, COL_TILE
    grp = jnp.arange(tn) // HEAD_DIM
    gsum = (grp[:, None] == grp[None, :]).astype(BF16)
    gq_t = jnp.tile(gq, tn // HEAD_DIM)[None, :]
    gk_t = jnp.tile(gk, tn // HEAD_DIM)[None, :]

    def wspec(off):
        return pl.BlockSpec((d, tn), lambda m, n, o=off // tn: (0, o + n))

    const = lambda shape: pl.BlockSpec(shape, lambda m, n: (0, 0))
    return pl.pallas_call(
        _qk_kernel,
        grid=(s // tm, D_ATTN // tn),
        in_specs=[
            pl.BlockSpec((tm, d), lambda m, n: (m, 0)),
            wspec(OFF_Q), wspec(OFF_K),
            const((tn, tn)), const((1, tn)), const((1, tn)),
            pl.BlockSpec((tm, LANES), lambda m, n: (m, 0)),
            pl.BlockSpec((tm, LANES), lambda m, n: (m, 0)),
        ],
        out_specs=[
            pl.BlockSpec((tn, tm), lambda m, n: (n, m)),
            pl.BlockSpec((tm, tn), lambda m, n: (m, n)),
        ],
        out_shape=[
            jax.ShapeDtypeStruct((D_ATTN, s), BF16),
            jax.ShapeDtypeStruct((s, D_ATTN), BF16),
        ],
        compiler_params=_params(("parallel", "parallel"), 48),
        name="qk_branch",
    )(h, w_in, w_in, gsum, gq_t, gk_t, cos, sin)


def _vt_kernel(h_ref, w_ref, o_ref):
    vt = _dot(h_ref[...], w_ref[...]).astype(BF16).T
    for hh in range(o_ref.shape[0]):
        for t in range(o_ref.shape[1]):
            o_ref[hh, t] = vt[hh * V_DIM:(hh + 1) * V_DIM, t * KV_TILE:(t + 1) * KV_TILE]


def _v_branch(h, w_in):
    s, d = h.shape
    tm, tn = ROW_TILE, COL_TILE
    hpt = tn // V_DIM
    return pl.pallas_call(
        _vt_kernel,
        grid=(s // tm, D_ATTN // tn),
        in_specs=[
            pl.BlockSpec((tm, d), lambda m, n: (m, 0)),
            pl.BlockSpec((d, tn), lambda m, n, o=OFF_V // tn: (0, o + n)),
        ],
        out_specs=pl.BlockSpec((hpt, tm // KV_TILE, V_DIM, KV_TILE),
                               lambda m, n: (n, m, 0, 0)),
        out_shape=jax.ShapeDtypeStruct((N_HEADS, s // KV_TILE, V_DIM, KV_TILE), BF16),
        compiler_params=_params(("parallel", "parallel"), 48),
        name="v_branch",
    )(h, w_in)


def _act_kernel(h_ref, w_ref, o_ref, *, act):
    r = _dot(h_ref[...], w_ref[...])
    sg = jax.nn.sigmoid(r)
    o_ref[...] = (r * sg if act == "silu" else sg).astype(BF16)


def _act_branch(h, w_in, off, width, act):
    s, d = h.shape
    tm, tn = ROW_TILE, 2 * COL_TILE
    return pl.pallas_call(
        functools.partial(_act_kernel, act=act),
        grid=(s // tm, width // tn),
        in_specs=[
            pl.BlockSpec((tm, d), lambda m, n: (m, 0)),
            pl.BlockSpec((d, tn), lambda m, n, o=off // tn: (0, o + n)),
        ],
        out_specs=pl.BlockSpec((tm, tn), lambda m, n: (m, n)),
        out_shape=jax.ShapeDtypeStruct((s, width), BF16),
        compiler_params=_params(("parallel", "parallel"), 48),
        name="proj_" + act,
    )(h, w_in)


def _attn_kernel(lamv_ref, qt_ref, k_ref, vt_ref, zs_ref, g_ref, o_ref,
                 s_ref, m_ref, acc_ref, *, lambda_init):
    tq = qt_ref.shape[1]
    tk = vt_ref.shape[2]
    i = pl.program_id(1)

    qt = qt_ref[...]
    comp1 = lax.broadcasted_iota(jnp.int32, qt.shape, 0) < HEAD_DIM
    zero = jnp.zeros_like(qt)
    qz = (jnp.where(comp1, qt, zero), jnp.where(comp1, zero, qt))
    ones = jnp.ones((BF16_SUBLANES, tk), BF16)

    def scores(c, j):
        kj = k_ref[pl.ds(pl.multiple_of(j * tk, tk), tk), :]
        s_ref[c] = _dot(kj, qz[c])

    def consume(c, j, mask):
        s = s_ref[c]
        if mask is not None:
            s = jnp.where(mask, s, -jnp.inf)
        m_old = m_ref[c]
        m_new = jnp.maximum(m_old, jnp.max(s, axis=0, keepdims=True))
        m_ref[c] = m_new
        p = jnp.exp2(s - m_new).astype(BF16)
        vj = jnp.concatenate([vt_ref[j], ones], axis=0)
        acc_ref[c] = acc_ref[c] * jnp.exp2(m_old - m_new) + _dot(vj, p)

    m_ref[...] = jnp.full(m_ref.shape, -jnp.inf, F32)
    acc_ref[...] = jnp.zeros(acc_ref.shape, F32)
    scores(0, 0)

    def body(t, carry):
        scores(1, t)
        consume(0, t, None)
        scores(0, t + 1)
        consume(1, t, None)
        return carry

    lax.fori_loop(0, i, body, 0)
    mask = (lax.broadcasted_iota(jnp.int32, (tk, tq), 0) // CHUNK
            <= lax.broadcasted_iota(jnp.int32, (tk, tq), 1) // CHUNK)
    scores(1, i)
    consume(0, i, mask)
    consume(1, i, mask)

    lamv = lamv_ref[...]
    lam = (jnp.exp(jnp.sum(lamv[0:1] * lamv[1:2], axis=-1, keepdims=True))
           - jnp.exp(jnp.sum(lamv[2:3] * lamv[3:4], axis=-1, keepdims=True))
           + lambda_init)
    o1 = acc_ref[0, :V_DIM, :] / acc_ref[0, V_DIM:V_DIM + 1, :]
    o2 = acc_ref[1, :V_DIM, :] / acc_ref[1, V_DIM:V_DIM + 1, :]
    d = (o1 - lam * o2).T
    dn = d * lax.rsqrt(jnp.mean(d * d, axis=-1, keepdims=True) + EPS) * g_ref[...]
    o_ref[...] = (dn * (1.0 - lambda_init) * zs_ref[...].astype(F32)).astype(BF16)


def _attention(lamv, qt, k, vt, zs, subln_g, lambda_init):
    s = k.shape[0]
    assert Q_TILE == KV_TILE and Q_TILE % CHUNK == 0
    return pl.pallas_call(
        functools.partial(_attn_kernel, lambda_init=lambda_init),
        grid=(N_HEADS, s // Q_TILE),
        in_specs=[
            pl.BlockSpec((4, HEAD_DIM), lambda h, i: (0, 0)),
            pl.BlockSpec((V_DIM, Q_TILE), lambda h, i: (h, i)),
            pl.BlockSpec((s, V_DIM), lambda h, i: (0, h)),
            pl.BlockSpec((None, s // KV_TILE, V_DIM, KV_TILE), lambda h, i: (h, 0, 0, 0)),
            pl.BlockSpec((Q_TILE, V_DIM), lambda h, i: (i, h)),
            pl.BlockSpec((1, V_DIM), lambda h, i: (0, 0)),
        ],
        out_specs=pl.BlockSpec((Q_TILE, V_DIM), lambda h, i: (i, h)),
        out_shape=jax.ShapeDtypeStruct((s, D_ATTN), BF16),
        scratch_shapes=[
            pltpu.VMEM((2, KV_TILE, Q_TILE), F32),
            pltpu.VMEM((2, 1, Q_TILE), F32),
            pltpu.VMEM((2, V_DIM + BF16_SUBLANES, Q_TILE), F32),
        ],
        compiler_params=_params(("parallel", "parallel"), 48),
        name="diff_attn",
    )(lamv, qt, k, vt, zs, subln_g)


def _out_kernel(yc_ref, oz_ref, ga_ref, gb_ref, wc_ref, wa_ref, wo_ref, x_ref, gate_ref,
                o_ref, merged_ref):
    @pl.when(pl.program_id(1) == 0)
    def _():
        a = _dot(yc_ref[...], wc_ref[...])
        b = _dot(oz_ref[...], wa_ref[...])
        merged_ref[...] = (ga_ref[...].astype(F32) * a
                           + gb_ref[...].astype(F32) * b).astype(BF16)

    o_ref[...] = x_ref[...] + gate_ref[...] * _dot(merged_ref[...], wo_ref[...])


def _merge_out(yc, oz, gates, wc, wa, wo, x, gate):
    s, d = x.shape
    tm, tn = ROW_TILE, OUT_COL_TILE
    return pl.pallas_call(
        _out_kernel,
        grid=(s // tm, d // tn),
        in_specs=[
            pl.BlockSpec((tm, D_CONV), lambda m, n: (m, 0)),
            pl.BlockSpec((tm, D_ATTN), lambda m, n: (m, 0)),
            pl.BlockSpec((tm, d), lambda m, n: (m, 0)),
            pl.BlockSpec((tm, d), lambda m, n: (m, 1)),
            pl.BlockSpec((D_CONV, d), lambda m, n: (0, 0)),
            pl.BlockSpec((D_ATTN, d), lambda m, n: (0, 0)),
            pl.BlockSpec((d, tn), lambda m, n: (0, n)),
            pl.BlockSpec((tm, tn), lambda m, n: (m, n)),
            pl.BlockSpec((1, tn), lambda m, n: (0, n)),
        ],
        out_specs=pl.BlockSpec((tm, tn), lambda m, n: (m, n)),
        out_shape=jax.ShapeDtypeStruct((s, d), F32),
        scratch_shapes=[pltpu.VMEM((tm, d), BF16)],
        compiler_params=_params(("parallel", "arbitrary"), 56),
        name="merge_out",
    )(yc, oz, gates, gates, wc, wa, wo, x, gate)


def kernel(x, c, ada_w, ada_b, norm_g, w_in, conv_w, w_conv_out, q_norm_g, k_norm_g,
           lam_q1, lam_k1, lam_q2, lam_k2, subln_g, w_attn_out, w_o):
    bsz, seq, d = x.shape
    depth = w_in.shape[0]
    assert bsz == 1 and d == D_MODEL and seq % ROW_TILE == 0 and seq % Q_TILE == 0
    xs = x[0]
    mod = _modulation(c, ada_w, ada_b)
    cos, sin = _rope_tables(seq)
    lamv = jnp.stack([lam_q1, lam_k1, lam_q2, lam_k2], axis=1).astype(F32)
    for l in range(depth):
        shift, scale, gate = (mod[l, :, i * d:(i + 1) * d] for i in range(3))
        w_l = w_in[l].astype(BF16)
        h = _norm_mod(xs, norm_g[l][None, :], scale, shift)
        yc = _conv_branch(h, w_l, conv_w[l])
        qt, k = _qk_branch(h, w_l, q_norm_g[l], k_norm_g[l], cos, sin)
        vt = _v_branch(h, w_l)
        zs = _act_branch(h, w_l, OFF_ZB, D_ATTN, "silu")
        gates = _act_branch(h, w_l, OFF_G, 2 * d, "sigmoid")
        lambda_init = 0.8 - 0.6 * math.exp(-0.3 * l)
        oz = _attention(lamv[l], qt, k, vt, zs, subln_g[l][None, :], lambda_init)
        xs = _merge_out(yc, oz, gates, w_conv_out[l].astype(BF16),
                        w_attn_out[l].astype(BF16), w_o[l].astype(BF16), xs, gate)
    return xs[None]
```

```python
import functools
import math

import jax
import jax.numpy as jnp
from jax import lax
from jax.experimental import pallas as pl
from jax.experimental.pallas import tpu as pltpu

F32 = jnp.float32
BF16 = jnp.bfloat16

D_MODEL = 2048
D_CONV = 1024
N_HEADS = 8
HEAD_DIM = 64
V_DIM = 2 * HEAD_DIM
D_ATTN = N_HEADS * V_DIM
CHUNK = 64
ROPE_THETA = 10000.0
EPS = 1e-6
LOG2_E = math.log2(math.e)
OFF_U, OFF_BG, OFF_CG, OFF_ZA = 0, D_CONV, 2 * D_CONV, 3 * D_CONV
OFF_Q = 4 * D_CONV
OFF_K = OFF_Q + D_ATTN
OFF_V = OFF_K + D_ATTN
OFF_ZB = OFF_V + D_ATTN
OFF_G = OFF_ZB + D_ATTN

LANES = 128
BF16_SUBLANES = 16
MIB = 1024 * 1024

ROW_TILE = 512
WIDE_ROW_TILE = 1024
COL_TILE = 512
Q_TILE = 512
KV_TILE = 512
MOD_COL_TILE = 768


def _params(semantics, vmem_mib):
    return pltpu.CompilerParams(dimension_semantics=semantics,
                                vmem_limit_bytes=vmem_mib * MIB)


def _dot(a, b):
    return jnp.dot(a, b, preferred_element_type=F32)


def _wspec(layer, rows, tn, off):
    return pl.BlockSpec((None, rows, tn), lambda m, n: (layer, 0, off // tn + n))


def _mod_kernel(c_ref, w_ref, b_ref, o_ref):
    c = c_ref[...]
    o_ref[0] = _dot(c * jax.nn.sigmoid(c), w_ref[0]) + b_ref[0]


def _modulation(c, ada_w, ada_b):
    depth, d, n = ada_w.shape
    c8 = jnp.broadcast_to(c, (8, d))
    out = pl.pallas_call(
        _mod_kernel,
        grid=(depth, n // MOD_COL_TILE),
        in_specs=[
            pl.BlockSpec((8, d), lambda l, j: (0, 0)),
            pl.BlockSpec((1, d, MOD_COL_TILE), lambda l, j: (l, 0, j)),
            pl.BlockSpec((1, 1, MOD_COL_TILE), lambda l, j: (l, 0, j)),
        ],
        out_specs=pl.BlockSpec((1, 8, MOD_COL_TILE), lambda l, j: (l, 0, j)),
        out_shape=jax.ShapeDtypeStruct((depth, 8, n), F32),
        compiler_params=_params(("parallel", "parallel"), 32),
        name="adaln_mod",
    )(c8, ada_w, ada_b.reshape(depth, 1, n))
    return out[:, 0:1, :]


def _norm_kernel(x_ref, g_ref, scale_ref, shift_ref, h_ref):
    x = x_ref[...]
    y = x * lax.rsqrt(jnp.mean(x * x, axis=-1, keepdims=True) + EPS) * g_ref[...]
    h_ref[...] = (y * (1.0 + scale_ref[...]) + shift_ref[...]).astype(BF16)


def _norm_mod(x, g, scale, shift):
    s, d = x.shape
    row = pl.BlockSpec((1, d), lambda m: (0, 0))
    return pl.pallas_call(
        _norm_kernel,
        grid=(s // ROW_TILE,),
        in_specs=[pl.BlockSpec((ROW_TILE, d), lambda m: (m, 0)), row, row, row],
        out_specs=pl.BlockSpec((ROW_TILE, d), lambda m: (m, 0)),
        out_shape=jax.ShapeDtypeStruct((s, d), BF16),
        compiler_params=_params(("parallel",), 32),
        name="norm_mod",
    )(x, g, scale, shift)


def _conv_kernel(h_ref, halo_ref, wu_ref, wb_ref, wc_ref, wz_ref, cw_ref, o_ref):
    h = h_ref[...]
    wu = wu_ref[...]
    wc = wc_ref[...]
    v = _dot(h, wc) * _dot(h, wu)
    halo = halo_ref[...]
    vh = _dot(halo, wc) * _dot(halo, wu)
    vh = jnp.where(pl.program_id(0) > 0, vh, 0.0)
    prev1 = vh[BF16_SUBLANES - 1:BF16_SUBLANES, :]
    prev2 = vh[BF16_SUBLANES - 2:BF16_SUBLANES - 1, :]
    row = lax.broadcasted_iota(jnp.int32, v.shape, 0)
    v1 = jnp.where(row == 0, prev1, pltpu.roll(v, 1, axis=0))
    v2 = jnp.where(row == 0, prev2,
                   jnp.where(row == 1, prev1, pltpu.roll(v, 2, axis=0)))
    cw = cw_ref[...]
    y = cw[0:1, :] * v2 + cw[1:2, :] * v1 + cw[2:3, :] * v
    za = _dot(h, wz_ref[...])
    o_ref[...] = (_dot(h, wb_ref[...]) * y * (za * jax.nn.sigmoid(za))).astype(BF16)


def _conv_branch(h, w_in, layer, conv_w):
    s, d = h.shape
    tm, tn = ROW_TILE, COL_TILE
    return pl.pallas_call(
        _conv_kernel,
        grid=(s // tm, D_CONV // tn),
        in_specs=[
            pl.BlockSpec((tm, d), lambda m, n: (m, 0)),
            pl.BlockSpec((BF16_SUBLANES, d),
                         lambda m, n: (jnp.maximum(m * (tm // BF16_SUBLANES) - 1, 0), 0)),
            _wspec(layer, d, tn, OFF_U), _wspec(layer, d, tn, OFF_BG),
            _wspec(layer, d, tn, OFF_CG), _wspec(layer, d, tn, OFF_ZA),
            pl.BlockSpec((3, tn), lambda m, n: (0, n)),
        ],
        out_specs=pl.BlockSpec((tm, tn), lambda m, n: (m, n)),
        out_shape=jax.ShapeDtypeStruct((s, D_CONV), BF16),
        compiler_params=_params(("parallel", "parallel"), 56),
        name="conv_branch",
    )(h, h, w_in, w_in, w_in, w_in, conv_w)


def _group_norm_rope(r, gsum, g, cos, sin):
    ss = _dot((r * r).astype(BF16), gsum)
    rn = r * lax.rsqrt(ss * (1.0 / HEAD_DIM) + EPS) * g
    half = HEAD_DIM // 2
    lane = lax.broadcasted_iota(jnp.int32, (r.shape[0], LANES), 1)
    first_half = (lane % HEAD_DIM) < half
    parts = []
    for a in range(r.shape[1] // LANES):
        x = rn[:, a * LANES:(a + 1) * LANES]
        partner = jnp.where(first_half, pltpu.roll(x, LANES - half, axis=1),
                            pltpu.roll(x, half, axis=1))
        parts.append(x * cos + partner * sin)
    return jnp.concatenate(parts, axis=1)


def _qk_kernel(h_ref, wq_ref, wk_ref, gsum_ref, gq_ref, gk_ref, cos_ref, sin_ref,
               qt_ref, k_ref):
    h = h_ref[...]
    gsum = gsum_ref[...]
    cos = cos_ref[...]
    sin = sin_ref[...]
    q = _group_norm_rope(_dot(h, wq_ref[...]), gsum, gq_ref[...], cos, sin)
    qt_ref[...] = (q * (HEAD_DIM ** -0.5 * LOG2_E)).T.astype(BF16)
    k = _group_norm_rope(_dot(h, wk_ref[...]), gsum, gk_ref[...], cos, sin)
    k_ref[...] = k.astype(BF16)


def _rope_tables(s):
    half = HEAD_DIM // 2
    inv = ROPE_THETA ** (-jnp.arange(half, dtype=F32) / half)
    ang = jnp.arange(s).astype(F32)[:, None] * inv[None, :]
    reps = LANES // half
    cos = jnp.tile(jnp.cos(ang), (1, reps))
    sign = jnp.where((jnp.arange(LANES) % HEAD_DIM) < half, -1.0, 1.0).astype(F32)
    sin = jnp.tile(jnp.sin(ang), (1, reps)) * sign[None, :]
    return cos, sin


def _qk_branch(h, w_in, layer, gq, gk, cos, sin):
    s, d = h.shape
    tm, tn = ROW_TILE, COL_TILE
    grp = jnp.arange(tn) // HEAD_DIM
    gsum = (grp[:, None] == grp[None, :]).astype(BF16)
    gq_t = jnp.tile(gq, tn // HEAD_DIM)[None, :]
    gk_t = jnp.tile(gk, tn // HEAD_DIM)[None, :]
    const = lambda shape: pl.BlockSpec(shape, lambda m, n: (0, 0))
    return pl.pallas_call(
        _qk_kernel,
        grid=(s // tm, D_ATTN // tn),
        in_specs=[
            pl.BlockSpec((tm, d), lambda m, n: (m, 0)),
            _wspec(layer, d, tn, OFF_Q), _wspec(layer, d, tn, OFF_K),
            const((tn, tn)), const((1, tn)), const((1, tn)),
            pl.BlockSpec((tm, LANES), lambda m, n: (m, 0)),
            pl.BlockSpec((tm, LANES), lambda m, n: (m, 0)),
        ],
        out_specs=[
            pl.BlockSpec((tn, tm), lambda m, n: (n, m)),
            pl.BlockSpec((tm, tn), lambda m, n: (m, n)),
        ],
        out_shape=[
            jax.ShapeDtypeStruct((D_ATTN, s), BF16),
            jax.ShapeDtypeStruct((s, D_ATTN), BF16),
        ],
        compiler_params=_params(("parallel", "parallel"), 56),
        name="qk_branch",
    )(h, w_in, w_in, gsum, gq_t, gk_t, cos, sin)


def _vt_kernel(h_ref, w_ref, o_ref):
    vt = _dot(h_ref[...], w_ref[...]).astype(BF16).T
    for hh in range(o_ref.shape[0]):
        for t in range(o_ref.shape[1]):
            o_ref[hh, t] = vt[hh * V_DIM:(hh + 1) * V_DIM, t * KV_TILE:(t + 1) * KV_TILE]


def _v_branch(h, w_in, layer):
    s, d = h.shape
    tm, tn = WIDE_ROW_TILE, COL_TILE
    return pl.pallas_call(
        _vt_kernel,
        grid=(s // tm, D_ATTN // tn),
        in_specs=[pl.BlockSpec((tm, d), lambda m, n: (m, 0)), _wspec(layer, d, tn, OFF_V)],
        out_specs=pl.BlockSpec((tn // V_DIM, tm // KV_TILE, V_DIM, KV_TILE),
                               lambda m, n: (n, m, 0, 0)),
        out_shape=jax.ShapeDtypeStruct((N_HEADS, s // KV_TILE, V_DIM, KV_TILE), BF16),
        compiler_params=_params(("parallel", "parallel"), 56),
        name="v_branch",
    )(h, w_in)


def _act_kernel(h_ref, w_ref, o_ref, *, act):
    r = _dot(h_ref[...], w_ref[...])
    sg = jax.nn.sigmoid(r)
    o_ref[...] = (r * sg if act == "silu" else sg).astype(BF16)


def _act_branch(h, w_in, layer, off, width, act):
    s, d = h.shape
    tm, tn = WIDE_ROW_TILE, 2 * COL_TILE
    return pl.pallas_call(
        functools.partial(_act_kernel, act=act),
        grid=(s // tm, width // tn),
        in_specs=[pl.BlockSpec((tm, d), lambda m, n: (m, 0)), _wspec(layer, d, tn, off)],
        out_specs=pl.BlockSpec((tm, tn), lambda m, n: (m, n)),
        out_shape=jax.ShapeDtypeStruct((s, width), BF16),
        compiler_params=_params(("parallel", "parallel"), 56),
        name="proj_" + act,
    )(h, w_in)


def _attn_kernel(lamv_ref, qt_ref, k_ref, vt_ref, zs_ref, g_ref, o_ref,
                 s_ref, m_ref, acc_ref, *, lambda_init):
    tq = qt_ref.shape[1]
    tk = vt_ref.shape[2]
    i = pl.program_id(1)

    qt = qt_ref[...]
    comp1 = lax.broadcasted_iota(jnp.int32, qt.shape, 0) < HEAD_DIM
    zero = jnp.zeros_like(qt)
    qz = (jnp.where(comp1, qt, zero), jnp.where(comp1, zero, qt))
    ones = jnp.ones((BF16_SUBLANES, tk), BF16)

    def scores(c, j):
        kj = k_ref[pl.ds(pl.multiple_of(j * tk, tk), tk), :]
        s_ref[c] = _dot(kj, qz[c])

    def consume(c, j, mask=None):
        s = s_ref[c]
        if mask is not None:
            s = jnp.where(mask, s, -jnp.inf)
        m_old = m_ref[c]
        m_new = jnp.maximum(m_old, jnp.max(s, axis=0, keepdims=True))
        m_ref[c] = m_new
        p = jnp.exp2(s - m_new).astype(BF16)
        vj = jnp.concatenate([vt_ref[j], ones], axis=0)
        acc_ref[c] = acc_ref[c] * jnp.exp2(m_old - m_new) + _dot(vj, p)

    def full_tile(t):
        scores(1, t)
        consume(0, t)
        scores(0, t + 1)
        consume(1, t)

    m_ref[...] = jnp.full(m_ref.shape, -jnp.inf, F32)
    acc_ref[...] = jnp.zeros(acc_ref.shape, F32)
    scores(0, 0)

    def two_tiles(u, carry):
        full_tile(2 * u)
        full_tile(2 * u + 1)
        return carry

    lax.fori_loop(0, jnp.right_shift(i, 1), two_tiles, 0)

    @pl.when(jnp.bitwise_and(i, 1) == 1)
    def _():
        full_tile(i - 1)

    mask = (lax.broadcasted_iota(jnp.int32, (tk, tq), 0) // CHUNK
            <= lax.broadcasted_iota(jnp.int32, (tk, tq), 1) // CHUNK)
    scores(1, i)
    consume(0, i, mask)
    consume(1, i, mask)

    lamv = lamv_ref[...]
    lam = (jnp.exp(jnp.sum(lamv[0:1] * lamv[1:2], axis=-1, keepdims=True))
           - jnp.exp(jnp.sum(lamv[2:3] * lamv[3:4], axis=-1, keepdims=True))
           + lambda_init)
    o1 = acc_ref[0, :V_DIM, :] / acc_ref[0, V_DIM:V_DIM + 1, :]
    o2 = acc_ref[1, :V_DIM, :] / acc_ref[1, V_DIM:V_DIM + 1, :]
    d = (o1 - lam * o2).T
    dn = d * lax.rsqrt(jnp.mean(d * d, axis=-1, keepdims=True) + EPS) * g_ref[...]
    o_ref[...] = (dn * (1.0 - lambda_init) * zs_ref[...].astype(F32)).astype(BF16)


def _attention(lamv, qt, k, vt, zs, subln_g, lambda_init):
    s = k.shape[0]
    assert Q_TILE == KV_TILE and Q_TILE % CHUNK == 0
    return pl.pallas_call(
        functools.partial(_attn_kernel, lambda_init=lambda_init),
        grid=(N_HEADS, s // Q_TILE),
        in_specs=[
            pl.BlockSpec((4, HEAD_DIM), lambda h, i: (0, 0)),
            pl.BlockSpec((V_DIM, Q_TILE), lambda h, i: (h, i)),
            pl.BlockSpec((s, V_DIM), lambda h, i: (0, h)),
            pl.BlockSpec((None, s // KV_TILE, V_DIM, KV_TILE), lambda h, i: (h, 0, 0, 0)),
            pl.BlockSpec((Q_TILE, V_DIM), lambda h, i: (i, h)),
            pl.BlockSpec((1, V_DIM), lambda h, i: (0, 0)),
        ],
        out_specs=pl.BlockSpec((Q_TILE, V_DIM), lambda h, i: (i, h)),
        out_shape=jax.ShapeDtypeStruct((s, D_ATTN), BF16),
        scratch_shapes=[
            pltpu.VMEM((2, KV_TILE, Q_TILE), F32),
            pltpu.VMEM((2, 1, Q_TILE), F32),
            pltpu.VMEM((2, V_DIM + BF16_SUBLANES, Q_TILE), F32),
        ],
        compiler_params=_params(("parallel", "parallel"), 48),
        name="diff_attn",
    )(lamv, qt, k, vt, zs, subln_g)


def _out_kernel(yc_ref, oz_ref, ga_ref, gb_ref, wc_ref, wa_ref, wo_ref, x_ref, gate_ref,
                o_ref):
    a = _dot(yc_ref[...], wc_ref[...])
    b = _dot(oz_ref[...], wa_ref[...])
    merged = (ga_ref[...].astype(F32) * a + gb_ref[...].astype(F32) * b).astype(BF16)
    o_ref[...] = x_ref[...] + gate_ref[...] * _dot(merged, wo_ref[...])


def _merge_out(yc, oz, gates, wc, wa, wo, layer, x, gate):
    s, d = x.shape
    tm = ROW_TILE
    resident = lambda rows: pl.BlockSpec((None, rows, d), lambda m: (layer, 0, 0),
                                         pipeline_mode=pl.Buffered(1))
    return pl.pallas_call(
        _out_kernel,
        grid=(s // tm,),
        in_specs=[
            pl.BlockSpec((tm, D_CONV), lambda m: (m, 0)),
            pl.BlockSpec((tm, D_ATTN), lambda m: (m, 0)),
            pl.BlockSpec((tm, d), lambda m: (m, 0)),
            pl.BlockSpec((tm, d), lambda m: (m, 1)),
            resident(D_CONV), resident(D_ATTN), resident(d),
            pl.BlockSpec((tm, d), lambda m: (m, 0)),
            pl.BlockSpec((1, d), lambda m: (0, 0)),
        ],
        out_specs=pl.BlockSpec((tm, d), lambda m: (m, 0)),
        out_shape=jax.ShapeDtypeStruct((s, d), F32),
        compiler_params=_params(("parallel",), 56),
        name="merge_out",
    )(yc, oz, gates, gates, wc, wa, wo, x, gate)


def kernel(x, c, ada_w, ada_b, norm_g, w_in, conv_w, w_conv_out, q_norm_g, k_norm_g,
           lam_q1, lam_k1, lam_q2, lam_k2, subln_g, w_attn_out, w_o):
    bsz, seq, d = x.shape
    depth = w_in.shape[0]
    assert bsz == 1 and d == D_MODEL and seq % WIDE_ROW_TILE == 0 and seq % Q_TILE == 0
    xs = x[0]
    mod = _modulation(c, ada_w, ada_b)
    cos, sin = _rope_tables(seq)
    w_in_b = w_in.astype(BF16)
    wc_b = w_conv_out.astype(BF16)
    wa_b = w_attn_out.astype(BF16)
    wo_b = w_o.astype(BF16)
    lamv = jnp.stack([lam_q1, lam_k1, lam_q2, lam_k2], axis=1).astype(F32)
    for l in range(depth):
        shift, scale, gate = (mod[l, :, i * d:(i + 1) * d] for i in range(3))
        h = _norm_mod(xs, norm_g[l][None, :], scale, shift)
        yc = _conv_branch(h, w_in_b, l, conv_w[l])
        qt, k = _qk_branch(h, w_in_b, l, q_norm_g[l], k_norm_g[l], cos, sin)
        vt = _v_branch(h, w_in_b, l)
        zs = _act_branch(h, w_in_b, l, OFF_ZB, D_ATTN, "silu")
        gates = _act_branch(h, w_in_b, l, OFF_G, 2 * d, "sigmoid")
        lambda_init = 0.8 - 0.6 * math.exp(-0.3 * l)
        oz = _attention(lamv[l], qt, k, vt, zs, subln_g[l][None, :], lambda_init)
        xs = _merge_out(yc, oz, gates, wc_b, wa_b, wo_b, l, xs, gate)
    return xs[None]
```

```python
import functools
import math

import jax
import jax.numpy as jnp
from jax import lax
from jax.experimental import pallas as pl
from jax.experimental.pallas import tpu as pltpu

F32 = jnp.float32
BF16 = jnp.bfloat16

D_MODEL = 2048
D_CONV = 1024
N_HEADS = 8
HEAD_DIM = 64
V_DIM = 2 * HEAD_DIM
D_ATTN = N_HEADS * V_DIM
CHUNK = 64
ROPE_THETA = 10000.0
EPS = 1e-6
LOG2_E = math.log2(math.e)
OFF_U, OFF_BG, OFF_CG, OFF_ZA = 0, D_CONV, 2 * D_CONV, 3 * D_CONV
OFF_Q = 4 * D_CONV
OFF_K = OFF_Q + D_ATTN
OFF_V = OFF_K + D_ATTN
OFF_ZB = OFF_V + D_ATTN
OFF_G = OFF_ZB + D_ATTN

LANES = 128
F32_SUBLANES = 8
BF16_SUBLANES = 16
MIB = 1024 * 1024

ROW_TILE = 512
WIDE_ROW_TILE = 1024
COL_TILE = 512
Q_TILE = 512
KV_TILE = 512
ATTN_HEADS_PER_STEP = 2
MOD_COL_TILE = 768


def _params(semantics, vmem_mib):
    return pltpu.CompilerParams(dimension_semantics=semantics,
                                vmem_limit_bytes=vmem_mib * MIB)


def _dot(a, b):
    return jnp.dot(a, b, preferred_element_type=F32)


def _wspec(layer, rows, tn, off):
    return pl.BlockSpec((None, rows, tn), lambda m, n: (layer, 0, off // tn + n))


def _mod_kernel(c_ref, w_ref, b_ref, o_ref):
    c = c_ref[...]
    o_ref[0] = _dot(c * jax.nn.sigmoid(c), w_ref[0]) + b_ref[0]


def _modulation(c, ada_w, ada_b):
    depth, d, n = ada_w.shape
    c8 = jnp.broadcast_to(c, (8, d))
    out = pl.pallas_call(
        _mod_kernel,
        grid=(depth, n // MOD_COL_TILE),
        in_specs=[
            pl.BlockSpec((8, d), lambda l, j: (0, 0)),
            pl.BlockSpec((1, d, MOD_COL_TILE), lambda l, j: (l, 0, j)),
            pl.BlockSpec((1, 1, MOD_COL_TILE), lambda l, j: (l, 0, j)),
        ],
        out_specs=pl.BlockSpec((1, 8, MOD_COL_TILE), lambda l, j: (l, 0, j)),
        out_shape=jax.ShapeDtypeStruct((depth, 8, n), F32),
        compiler_params=_params(("parallel", "parallel"), 32),
        name="adaln_mod",
    )(c8, ada_w, ada_b.reshape(depth, 1, n))
    return out[:, 0:1, :]


def _norm_kernel(x_ref, g_ref, scale_ref, shift_ref, h_ref):
    x = x_ref[...]
    y = x * lax.rsqrt(jnp.mean(x * x, axis=-1, keepdims=True) + EPS) * g_ref[...]
    h_ref[...] = (y * (1.0 + scale_ref[...]) + shift_ref[...]).astype(BF16)


def _norm_mod(x, g, scale, shift):
    s, d = x.shape
    row = pl.BlockSpec((1, d), lambda m: (0, 0))
    return pl.pallas_call(
        _norm_kernel,
        grid=(s // ROW_TILE,),
        in_specs=[pl.BlockSpec((ROW_TILE, d), lambda m: (m, 0)), row, row, row],
        out_specs=pl.BlockSpec((ROW_TILE, d), lambda m: (m, 0)),
        out_shape=jax.ShapeDtypeStruct((s, d), BF16),
        compiler_params=_params(("parallel",), 32),
        name="norm_mod",
    )(x, g, scale, shift)


def _conv_kernel(h_ref, wu_ref, wb_ref, wc_ref, wz_ref, cw_ref, o_ref, tail_ref):
    n = pl.program_id(1)
    h = h_ref[...]
    v = _dot(h, wc_ref[...]) * _dot(h, wu_ref[...])

    @pl.when(pl.program_id(0) == 0)
    def _():
        tail_ref[n] = jnp.zeros(tail_ref.shape[1:], F32)

    tail = tail_ref[n]
    tail_ref[n] = v[v.shape[0] - F32_SUBLANES:, :]
    prev1 = tail[F32_SUBLANES - 1:F32_SUBLANES, :]
    prev2 = tail[F32_SUBLANES - 2:F32_SUBLANES - 1, :]
    row = lax.broadcasted_iota(jnp.int32, v.shape, 0)
    v1 = jnp.where(row == 0, prev1, pltpu.roll(v, 1, axis=0))
    v2 = jnp.where(row == 0, prev2,
                   jnp.where(row == 1, prev1, pltpu.roll(v, 2, axis=0)))
    cw = cw_ref[...]
    y = cw[0:1, :] * v2 + cw[1:2, :] * v1 + cw[2:3, :] * v
    za = _dot(h, wz_ref[...])
    o_ref[...] = (_dot(h, wb_ref[...]) * y * (za * jax.nn.sigmoid(za))).astype(BF16)


def _conv_branch(h, w_in, layer, conv_w):
    s, d = h.shape
    tm, tn = ROW_TILE, COL_TILE
    return pl.pallas_call(
        _conv_kernel,
        grid=(s // tm, D_CONV // tn),
        in_specs=[
            pl.BlockSpec((tm, d), lambda m, n: (m, 0)),
            _wspec(layer, d, tn, OFF_U), _wspec(layer, d, tn, OFF_BG),
            _wspec(layer, d, tn, OFF_CG), _wspec(layer, d, tn, OFF_ZA),
            pl.BlockSpec((3, tn), lambda m, n: (0, n)),
        ],
        out_specs=pl.BlockSpec((tm, tn), lambda m, n: (m, n)),
        out_shape=jax.ShapeDtypeStruct((s, D_CONV), BF16),
        scratch_shapes=[pltpu.VMEM((D_CONV // tn, F32_SUBLANES, tn), F32)],
        compiler_params=_params(("arbitrary", "arbitrary"), 56),
        name="conv_branch",
    )(h, w_in, w_in, w_in, w_in, conv_w)


def _group_norm_rope(r, gsum, g, cos, sin):
    ss = _dot((r * r).astype(BF16), gsum)
    rn = r * lax.rsqrt(ss * (1.0 / HEAD_DIM) + EPS) * g
    half = HEAD_DIM // 2
    lane = lax.broadcasted_iota(jnp.int32, (r.shape[0], LANES), 1)
    first_half = (lane % HEAD_DIM) < half
    parts = []
    for a in range(r.shape[1] // LANES):
        x = rn[:, a * LANES:(a + 1) * LANES]
        partner = jnp.where(first_half, pltpu.roll(x, LANES - half, axis=1),
                            pltpu.roll(x, half, axis=1))
        parts.append(x * cos + partner * sin)
    return jnp.concatenate(parts, axis=1)


def _qk_kernel(h_ref, wq_ref, wk_ref, gsum_ref, gq_ref, gk_ref, cos_ref, sin_ref,
               qt_ref, k_ref):
    h = h_ref[...]
    gsum = gsum_ref[...]
    cos = cos_ref[...]
    sin = sin_ref[...]
    q = _group_norm_rope(_dot(h, wq_ref[...]), gsum, gq_ref[...], cos, sin)
    qt_ref[...] = (q * (HEAD_DIM ** -0.5 * LOG2_E)).T.astype(BF16)
    k = _group_norm_rope(_dot(h, wk_ref[...]), gsum, gk_ref[...], cos, sin)
    k_ref[...] = k.astype(BF16)


def _rope_tables(s):
    half = HEAD_DIM // 2
    inv = ROPE_THETA ** (-jnp.arange(half, dtype=F32) / half)
    ang = jnp.arange(s).astype(F32)[:, None] * inv[None, :]
    reps = LANES // half
    cos = jnp.tile(jnp.cos(ang), (1, reps))
    sign = jnp.where((jnp.arange(LANES) % HEAD_DIM) < half, -1.0, 1.0).astype(F32)
    sin = jnp.tile(jnp.sin(ang), (1, reps)) * sign[None, :]
    return cos, sin


def _qk_branch(h, w_in, layer, gq, gk, cos, sin):
    s, d = h.shape
    tm, tn = ROW_TILE, COL_TILE
    grp = jnp.arange(tn) // HEAD_DIM
    gsum = (grp[:, None] == grp[None, :]).astype(BF16)
    gq_t = jnp.tile(gq, tn // HEAD_DIM)[None, :]
    gk_t = jnp.tile(gk, tn // HEAD_DIM)[None, :]
    const = lambda shape: pl.BlockSpec(shape, lambda m, n: (0, 0))
    return pl.pallas_call(
        _qk_kernel,
        grid=(s // tm, D_ATTN // tn),
        in_specs=[
            pl.BlockSpec((tm, d), lambda m, n: (m, 0)),
            _wspec(layer, d, tn, OFF_Q), _wspec(layer, d, tn, OFF_K),
            const((tn, tn)), const((1, tn)), const((1, tn)),
            pl.BlockSpec((tm, LANES), lambda m, n: (m, 0)),
            pl.BlockSpec((tm, LANES), lambda m, n: (m, 0)),
        ],
        out_specs=[
            pl.BlockSpec((tn, tm), lambda m, n: (n, m)),
            pl.BlockSpec((tm, tn), lambda m, n: (m, n)),
        ],
        out_shape=[
            jax.ShapeDtypeStruct((D_ATTN, s), BF16),
            jax.ShapeDtypeStruct((s, D_ATTN), BF16),
        ],
        compiler_params=_params(("parallel", "parallel"), 56),
        name="qk_branch",
    )(h, w_in, w_in, gsum, gq_t, gk_t, cos, sin)


def _vt_kernel(h_ref, w_ref, o_ref):
    vt = _dot(h_ref[...], w_ref[...]).astype(BF16).T
    for hh in range(o_ref.shape[0]):
        for t in range(o_ref.shape[1]):
            o_ref[hh, t] = vt[hh * V_DIM:(hh + 1) * V_DIM, t * KV_TILE:(t + 1) * KV_TILE]


def _v_branch(h, w_in, layer):
    s, d = h.shape
    tm, tn = WIDE_ROW_TILE, COL_TILE
    return pl.pallas_call(
        _vt_kernel,
        grid=(s // tm, D_ATTN // tn),
        in_specs=[pl.BlockSpec((tm, d), lambda m, n: (m, 0)), _wspec(layer, d, tn, OFF_V)],
        out_specs=pl.BlockSpec((tn // V_DIM, tm // KV_TILE, V_DIM, KV_TILE),
                               lambda m, n: (n, m, 0, 0)),
        out_shape=jax.ShapeDtypeStruct((N_HEADS, s // KV_TILE, V_DIM, KV_TILE), BF16),
        compiler_params=_params(("parallel", "parallel"), 56),
        name="v_branch",
    )(h, w_in)


def _act_kernel(h_ref, w_ref, o_ref, *, act):
    r = _dot(h_ref[...], w_ref[...])
    sg = jax.nn.sigmoid(r)
    o_ref[...] = (r * sg if act == "silu" else sg).astype(BF16)


def _act_branch(h, w_in, layer, off, width, act):
    s, d = h.shape
    tm, tn = WIDE_ROW_TILE, 2 * COL_TILE
    return pl.pallas_call(
        functools.partial(_act_kernel, act=act),
        grid=(s // tm, width // tn),
        in_specs=[pl.BlockSpec((tm, d), lambda m, n: (m, 0)), _wspec(layer, d, tn, off)],
        out_specs=pl.BlockSpec((tm, tn), lambda m, n: (m, n)),
        out_shape=jax.ShapeDtypeStruct((s, width), BF16),
        compiler_params=_params(("parallel", "parallel"), 56),
        name="proj_" + act,
    )(h, w_in)


def _attn_kernel(lamv_ref, qt_ref, k_ref, vt_ref, zs_ref, g_ref, o_ref,
                 s_ref, smax_ref, m_ref, acc_ref, *, lambda_init):
    tq = qt_ref.shape[1]
    nh, _, _, tk = vt_ref.shape
    i = pl.program_id(1)

    comp1 = lax.broadcasted_iota(jnp.int32, (V_DIM, tq), 0) < HEAD_DIM
    qz = {}
    for hh in range(nh):
        qt = qt_ref[hh * V_DIM:(hh + 1) * V_DIM, :]
        zero = jnp.zeros_like(qt)
        qz[hh, 0] = jnp.where(comp1, qt, zero)
        qz[hh, 1] = jnp.where(comp1, zero, qt)
    ones = jnp.ones((BF16_SUBLANES, tk), BF16)
    mask = (lax.broadcasted_iota(jnp.int32, (tk, tq), 0) // CHUNK
            <= lax.broadcasted_iota(jnp.int32, (tk, tq), 1) // CHUNK)
    heads = range(nh)

    def scores(hh, c, j, masked=False):
        kj = k_ref[pl.ds(pl.multiple_of(j * tk, tk), tk), hh * V_DIM:(hh + 1) * V_DIM]
        s = _dot(kj, qz[hh, c])
        if masked:
            s = jnp.where(mask, s, -jnp.inf)
        s_ref[2 * hh + c] = s
        smax_ref[2 * hh + c] = jnp.max(s, axis=0, keepdims=True)

    def consume(hh, c, j, remask=False):
        ci = 2 * hh + c
        s = s_ref[ci]
        smax = smax_ref[ci]
        if remask:
            s = jnp.where(mask, s, -jnp.inf)
            smax = jnp.max(s, axis=0, keepdims=True)
        m_old = m_ref[ci]
        m_new = jnp.maximum(m_old, smax)
        m_ref[ci] = m_new
        p = jnp.exp2(s - m_new).astype(BF16)
        vj = jnp.concatenate([vt_ref[hh, j], ones], axis=0)
        acc_ref[ci] = acc_ref[ci] * jnp.exp2(m_old - m_new) + _dot(vj, p)

    def full_tile(t):
        for hh in heads:
            scores(hh, 1, t)
        for hh in heads:
            consume(hh, 0, t)
        for hh in heads:
            scores(hh, 0, t + 1)
        for hh in heads:
            consume(hh, 1, t)

    m_ref[...] = jnp.full(m_ref.shape, -jnp.inf, F32)
    acc_ref[...] = jnp.zeros(acc_ref.shape, F32)
    for hh in heads:
        scores(hh, 0, 0)

    def two_tiles(u, carry):
        full_tile(2 * u)
        full_tile(2 * u + 1)
        return carry

    lax.fori_loop(0, jnp.right_shift(i, 1), two_tiles, 0)

    @pl.when(jnp.bitwise_and(i, 1) == 1)
    def _():
        full_tile(i - 1)

    for hh in heads:
        scores(hh, 1, i, masked=True)
    for hh in heads:
        consume(hh, 0, i, remask=True)
    for hh in heads:
        consume(hh, 1, i)

    lamv = lamv_ref[...]
    lam = (jnp.exp(jnp.sum(lamv[0:1] * lamv[1:2], axis=-1, keepdims=True))
           - jnp.exp(jnp.sum(lamv[2:3] * lamv[3:4], axis=-1, keepdims=True))
           + lambda_init)
    for hh in heads:
        o1 = acc_ref[2 * hh, :V_DIM, :] / acc_ref[2 * hh, V_DIM:V_DIM + 1, :]
        o2 = acc_ref[2 * hh + 1, :V_DIM, :] / acc_ref[2 * hh + 1, V_DIM:V_DIM + 1, :]
        d = (o1 - lam * o2).T
        dn = d * lax.rsqrt(jnp.mean(d * d, axis=-1, keepdims=True) + EPS) * g_ref[...]
        cols = slice(hh * V_DIM, (hh + 1) * V_DIM)
        o_ref[:, cols] = (dn * (1.0 - lambda_init)
                          * zs_ref[:, cols].astype(F32)).astype(BF16)


def _attention(lamv, qt, k, vt, zs, subln_g, lambda_init):
    s = k.shape[0]
    assert Q_TILE == KV_TILE and Q_TILE % CHUNK == 0
    nh = ATTN_HEADS_PER_STEP
    chains = 2 * nh
    return pl.pallas_call(
        functools.partial(_attn_kernel, lambda_init=lambda_init),
        grid=(N_HEADS // nh, s // Q_TILE),
        in_specs=[
            pl.BlockSpec((4, HEAD_DIM), lambda g, i: (0, 0)),
            pl.BlockSpec((nh * V_DIM, Q_TILE), lambda g, i: (g, i)),
            pl.BlockSpec((s, nh * V_DIM), lambda g, i: (0, g)),
            pl.BlockSpec((nh, s // KV_TILE, V_DIM, KV_TILE), lambda g, i: (g, 0, 0, 0)),
            pl.BlockSpec((Q_TILE, nh * V_DIM), lambda g, i: (i, g)),
            pl.BlockSpec((1, V_DIM), lambda g, i: (0, 0)),
        ],
        out_specs=pl.BlockSpec((Q_TILE, nh * V_DIM), lambda g, i: (i, g)),
        out_shape=jax.ShapeDtypeStruct((s, D_ATTN), BF16),
        scratch_shapes=[
            pltpu.VMEM((chains, KV_TILE, Q_TILE), F32),
            pltpu.VMEM((chains, 1, Q_TILE), F32),
            pltpu.VMEM((chains, 1, Q_TILE), F32),
            pltpu.VMEM((chains, V_DIM + BF16_SUBLANES, Q_TILE), F32),
        ],
        compiler_params=_params(("parallel", "parallel"), 56),
        name="diff_attn",
    )(lamv, qt, k, vt, zs, subln_g)


def _out_kernel(yc_ref, oz_ref, ga_ref, gb_ref, wc_ref, wa_ref, wo_ref, x_ref, gate_ref,
                o_ref):
    a = _dot(yc_ref[...], wc_ref[...])
    b = _dot(oz_ref[...], wa_ref[...])
    merged = (ga_ref[...].astype(F32) * a + gb_ref[...].astype(F32) * b).astype(BF16)
    o_ref[...] = x_ref[...] + gate_ref[...] * _dot(merged, wo_ref[...])


def _merge_out(yc, oz, gates, wc, wa, wo, layer, x, gate):
    s, d = x.shape
    tm = ROW_TILE
    resident = lambda rows: pl.BlockSpec((None, rows, d), lambda m: (layer, 0, 0),
                                         pipeline_mode=pl.Buffered(1))
    return pl.pallas_call(
        _out_kernel,
        grid=(s // tm,),
        in_specs=[
            pl.BlockSpec((tm, D_CONV), lambda m: (m, 0)),
            pl.BlockSpec((tm, D_ATTN), lambda m: (m, 0)),
            pl.BlockSpec((tm, d), lambda m: (m, 0)),
            pl.BlockSpec((tm, d), lambda m: (m, 1)),
            resident(D_CONV), resident(D_ATTN), resident(d),
            pl.BlockSpec((tm, d), lambda m: (m, 0)),
            pl.BlockSpec((1, d), lambda m: (0, 0)),
        ],
        out_specs=pl.BlockSpec((tm, d), lambda m: (m, 0)),
        out_shape=jax.ShapeDtypeStruct((s, d), F32),
        compiler_params=_params(("parallel",), 56),
        name="merge_out",
    )(yc, oz, gates, gates, wc, wa, wo, x, gate)


def kernel(x, c, ada_w, ada_b, norm_g, w_in, conv_w, w_conv_out, q_norm_g, k_norm_g,
           lam_q1, lam_k1, lam_q2, lam_k2, subln_g, w_attn_out, w_o):
    bsz, seq, d = x.shape
    depth = w_in.shape[0]
    assert bsz == 1 and d == D_MODEL and seq % WIDE_ROW_TILE == 0 and seq % Q_TILE == 0
    xs = x[0]
    mod = _modulation(c, ada_w, ada_b)
    cos, sin = _rope_tables(seq)
    w_in_b = w_in.astype(BF16)
    wc_b = w_conv_out.astype(BF16)
    wa_b = w_attn_out.astype(BF16)
    wo_b = w_o.astype(BF16)
    lamv = jnp.stack([lam_q1, lam_k1, lam_q2, lam_k2], axis=1).astype(F32)
    for l in range(depth):
        shift, scale, gate = (mod[l, :, i * d:(i + 1) * d] for i in range(3))
        h = _norm_mod(xs, norm_g[l][None, :], scale, shift)
        yc = _conv_branch(h, w_in_b, l, conv_w[l])
        qt, k = _qk_branch(h, w_in_b, l, q_norm_g[l], k_norm_g[l], cos, sin)
        vt = _v_branch(h, w_in_b, l)
        zs = _act_branch(h, w_in_b, l, OFF_ZB, D_ATTN, "silu")
        gates = _act_branch(h, w_in_b, l, OFF_G, 2 * d, "sigmoid")
        lambda_init = 0.8 - 0.6 * math.exp(-0.3 * l)
        oz = _attention(lamv[l], qt, k, vt, zs, subln_g[l][None, :], lambda_init)
        xs = _merge_out(yc, oz, gates, wc_b, wa_b, wo_b, l, xs, gate)
    return xs[None]
```

```python
import functools
import math

import jax
import jax.numpy as jnp
from jax import lax
from jax.experimental import pallas as pl
from jax.experimental.pallas import tpu as pltpu

F32 = jnp.float32
BF16 = jnp.bfloat16

D_MODEL = 2048
D_CONV = 1024
N_HEADS = 8
HEAD_DIM = 64
V_DIM = 2 * HEAD_DIM
D_ATTN = N_HEADS * V_DIM
CHUNK = 64
ROPE_THETA = 10000.0
EPS = 1e-6
LOG2_E = math.log2(math.e)
OFF_U, OFF_BG, OFF_CG, OFF_ZA = 0, D_CONV, 2 * D_CONV, 3 * D_CONV
OFF_Q = 4 * D_CONV
OFF_K = OFF_Q + D_ATTN
OFF_V = OFF_K + D_ATTN
OFF_ZB = OFF_V + D_ATTN
OFF_G = OFF_ZB + D_ATTN

LANES = 128
F32_SUBLANES = 8
BF16_SUBLANES = 16
MIB = 1024 * 1024

ROW_TILE = 512
WIDE_ROW_TILE = 1024
COL_TILE = 512
Q_TILE = 512
KV_TILE = 512
ATTN_HEADS_PER_STEP = 2
MOD_COL_TILE = 768


def _params(semantics, vmem_mib):
    return pltpu.CompilerParams(dimension_semantics=semantics,
                                vmem_limit_bytes=vmem_mib * MIB)


def _dot(a, b):
    return jnp.dot(a, b, preferred_element_type=F32)


def _wspec(layer, rows, tn, off):
    return pl.BlockSpec((None, rows, tn), lambda n, m: (layer, 0, off // tn + n))


def _cast_weights_once(pairs):
    @pl.when(pl.program_id(1) == 0)
    def _():
        for w_ref, wb_ref in pairs:
            wb_ref[...] = w_ref[...].astype(BF16)


def _mod_kernel(c_ref, w_ref, b_ref, o_ref):
    c = c_ref[...]
    o_ref[0] = _dot(c * jax.nn.sigmoid(c), w_ref[0]) + b_ref[0]


def _modulation(c, ada_w, ada_b):
    depth, d, n = ada_w.shape
    c8 = jnp.broadcast_to(c, (8, d))
    out = pl.pallas_call(
        _mod_kernel,
        grid=(depth, n // MOD_COL_TILE),
        in_specs=[
            pl.BlockSpec((8, d), lambda l, j: (0, 0)),
            pl.BlockSpec((1, d, MOD_COL_TILE), lambda l, j: (l, 0, j)),
            pl.BlockSpec((1, 1, MOD_COL_TILE), lambda l, j: (l, 0, j)),
        ],
        out_specs=pl.BlockSpec((1, 8, MOD_COL_TILE), lambda l, j: (l, 0, j)),
        out_shape=jax.ShapeDtypeStruct((depth, 8, n), F32),
        compiler_params=_params(("parallel", "parallel"), 32),
        name="adaln_mod",
    )(c8, ada_w, ada_b.reshape(depth, 1, n))
    return out[:, 0:1, :]


def _norm_kernel(x_ref, g_ref, scale_ref, shift_ref, h_ref):
    x = x_ref[...]
    y = x * lax.rsqrt(jnp.mean(x * x, axis=-1, keepdims=True) + EPS) * g_ref[...]
    h_ref[...] = (y * (1.0 + scale_ref[...]) + shift_ref[...]).astype(BF16)


def _norm_mod(x, g, scale, shift):
    s, d = x.shape
    row = pl.BlockSpec((1, d), lambda m: (0, 0))
    return pl.pallas_call(
        _norm_kernel,
        grid=(s // ROW_TILE,),
        in_specs=[pl.BlockSpec((ROW_TILE, d), lambda m: (m, 0)), row, row, row],
        out_specs=pl.BlockSpec((ROW_TILE, d), lambda m: (m, 0)),
        out_shape=jax.ShapeDtypeStruct((s, d), BF16),
        compiler_params=_params(("parallel",), 32),
        name="norm_mod",
    )(x, g, scale, shift)


def _conv_kernel(h_ref, wu_ref, wb_ref, wc_ref, wz_ref, cw_ref, o_ref,
                 wu_b, wb_b, wc_b, wz_b, tail_ref):
    _cast_weights_once([(wu_ref, wu_b), (wb_ref, wb_b), (wc_ref, wc_b), (wz_ref, wz_b)])
    h = h_ref[...]
    v = _dot(h, wc_b[...]) * _dot(h, wu_b[...])

    @pl.when(pl.program_id(1) == 0)
    def _():
        tail_ref[...] = jnp.zeros(tail_ref.shape, F32)

    tail = tail_ref[...]
    tail_ref[...] = v[v.shape[0] - F32_SUBLANES:, :]
    prev1 = tail[F32_SUBLANES - 1:F32_SUBLANES, :]
    prev2 = tail[F32_SUBLANES - 2:F32_SUBLANES - 1, :]
    row = lax.broadcasted_iota(jnp.int32, v.shape, 0)
    v1 = jnp.where(row == 0, prev1, pltpu.roll(v, 1, axis=0))
    v2 = jnp.where(row == 0, prev2,
                   jnp.where(row == 1, prev1, pltpu.roll(v, 2, axis=0)))
    cw = cw_ref[...]
    y = cw[0:1, :] * v2 + cw[1:2, :] * v1 + cw[2:3, :] * v
    za = _dot(h, wz_b[...])
    o_ref[...] = (_dot(h, wb_b[...]) * y * (za * jax.nn.sigmoid(za))).astype(BF16)


def _conv_branch(h, w_in, layer, conv_w):
    s, d = h.shape
    tm, tn = ROW_TILE, COL_TILE
    return pl.pallas_call(
        _conv_kernel,
        grid=(D_CONV // tn, s // tm),
        in_specs=[
            pl.BlockSpec((tm, d), lambda n, m: (m, 0)),
            _wspec(layer, d, tn, OFF_U), _wspec(layer, d, tn, OFF_BG),
            _wspec(layer, d, tn, OFF_CG), _wspec(layer, d, tn, OFF_ZA),
            pl.BlockSpec((3, tn), lambda n, m: (0, n)),
        ],
        out_specs=pl.BlockSpec((tm, tn), lambda n, m: (m, n)),
        out_shape=jax.ShapeDtypeStruct((s, D_CONV), BF16),
        scratch_shapes=[pltpu.VMEM((d, tn), BF16)] * 4
                       + [pltpu.VMEM((F32_SUBLANES, tn), F32)],
        compiler_params=_params(("arbitrary", "arbitrary"), 60),
        name="conv_branch",
    )(h, w_in, w_in, w_in, w_in, conv_w)


def _group_norm_rope(r, gsum, g, cos, sin):
    ss = _dot((r * r).astype(BF16), gsum)
    rn = r * lax.rsqrt(ss * (1.0 / HEAD_DIM) + EPS) * g
    half = HEAD_DIM // 2
    lane = lax.broadcasted_iota(jnp.int32, (r.shape[0], LANES), 1)
    first_half = (lane % HEAD_DIM) < half
    parts = []
    for a in range(r.shape[1] // LANES):
        x = rn[:, a * LANES:(a + 1) * LANES]
        partner = jnp.where(first_half, pltpu.roll(x, LANES - half, axis=1),
                            pltpu.roll(x, half, axis=1))
        parts.append(x * cos + partner * sin)
    return jnp.concatenate(parts, axis=1)


def _qk_kernel(h_ref, wq_ref, wk_ref, gsum_ref, gq_ref, gk_ref, cos_ref, sin_ref,
               qt_ref, k_ref, wq_b, wk_b):
    _cast_weights_once([(wq_ref, wq_b), (wk_ref, wk_b)])
    h = h_ref[...]
    gsum = gsum_ref[...]
    cos = cos_ref[...]
    sin = sin_ref[...]
    q = _group_norm_rope(_dot(h, wq_b[...]), gsum, gq_ref[...], cos, sin)
    qt_ref[...] = (q * (HEAD_DIM ** -0.5 * LOG2_E)).T.astype(BF16)
    k = _group_norm_rope(_dot(h, wk_b[...]), gsum, gk_ref[...], cos, sin)
    k_ref[...] = k.astype(BF16)


def _rope_tables(s):
    half = HEAD_DIM // 2
    inv = ROPE_THETA ** (-jnp.arange(half, dtype=F32) / half)
    ang = jnp.arange(s).astype(F32)[:, None] * inv[None, :]
    reps = LANES // half
    cos = jnp.tile(jnp.cos(ang), (1, reps))
    sign = jnp.where((jnp.arange(LANES) % HEAD_DIM) < half, -1.0, 1.0).astype(F32)
    sin = jnp.tile(jnp.sin(ang), (1, reps)) * sign[None, :]
    return cos, sin


def _qk_branch(h, w_in, layer, gq, gk, cos, sin):
    s, d = h.shape
    tm, tn = ROW_TILE, COL_TILE
    grp = jnp.arange(tn) // HEAD_DIM
    gsum = (grp[:, None] == grp[None, :]).astype(BF16)
    gq_t = jnp.tile(gq, tn // HEAD_DIM)[None, :]
    gk_t = jnp.tile(gk, tn // HEAD_DIM)[None, :]
    const = lambda shape: pl.BlockSpec(shape, lambda n, m: (0, 0))
    return pl.pallas_call(
        _qk_kernel,
        grid=(D_ATTN // tn, s // tm),
        in_specs=[
            pl.BlockSpec((tm, d), lambda n, m: (m, 0)),
            _wspec(layer, d, tn, OFF_Q), _wspec(layer, d, tn, OFF_K),
            const((tn, tn)), const((1, tn)), const((1, tn)),
            pl.BlockSpec((tm, LANES), lambda n, m: (m, 0)),
            pl.BlockSpec((tm, LANES), lambda n, m: (m, 0)),
        ],
        out_specs=[
            pl.BlockSpec((tn, tm), lambda n, m: (n, m)),
            pl.BlockSpec((tm, tn), lambda n, m: (m, n)),
        ],
        out_shape=[
            jax.ShapeDtypeStruct((D_ATTN, s), BF16),
            jax.ShapeDtypeStruct((s, D_ATTN), BF16),
        ],
        scratch_shapes=[pltpu.VMEM((d, tn), BF16)] * 2,
        compiler_params=_params(("arbitrary", "arbitrary"), 56),
        name="qk_branch",
    )(h, w_in, w_in, gsum, gq_t, gk_t, cos, sin)


def _vt_kernel(h_ref, w_ref, o_ref, w_b):
    _cast_weights_once([(w_ref, w_b)])
    vt = _dot(h_ref[...], w_b[...]).astype(BF16).T
    for hh in range(o_ref.shape[0]):
        for t in range(o_ref.shape[1]):
            o_ref[hh, t] = vt[hh * V_DIM:(hh + 1) * V_DIM, t * KV_TILE:(t + 1) * KV_TILE]


def _v_branch(h, w_in, layer):
    s, d = h.shape
    tm, tn = WIDE_ROW_TILE, COL_TILE
    return pl.pallas_call(
        _vt_kernel,
        grid=(D_ATTN // tn, s // tm),
        in_specs=[pl.BlockSpec((tm, d), lambda n, m: (m, 0)), _wspec(layer, d, tn, OFF_V)],
        out_specs=pl.BlockSpec((tn // V_DIM, tm // KV_TILE, V_DIM, KV_TILE),
                               lambda n, m: (n, m, 0, 0)),
        out_shape=jax.ShapeDtypeStruct((N_HEADS, s // KV_TILE, V_DIM, KV_TILE), BF16),
        scratch_shapes=[pltpu.VMEM((d, tn), BF16)],
        compiler_params=_params(("arbitrary", "arbitrary"), 56),
        name="v_branch",
    )(h, w_in)


def _act_kernel(h_ref, w_ref, o_ref, w_b, *, act):
    _cast_weights_once([(w_ref, w_b)])
    r = _dot(h_ref[...], w_b[...])
    sg = jax.nn.sigmoid(r)
    o_ref[...] = (r * sg if act == "silu" else sg).astype(BF16)


def _act_branch(h, w_in, layer, off, width, act):
    s, d = h.shape
    tm, tn = WIDE_ROW_TILE, 2 * COL_TILE
    return pl.pallas_call(
        functools.partial(_act_kernel, act=act),
        grid=(width // tn, s // tm),
        in_specs=[pl.BlockSpec((tm, d), lambda n, m: (m, 0)), _wspec(layer, d, tn, off)],
        out_specs=pl.BlockSpec((tm, tn), lambda n, m: (m, n)),
        out_shape=jax.ShapeDtypeStruct((s, width), BF16),
        scratch_shapes=[pltpu.VMEM((d, tn), BF16)],
        compiler_params=_params(("arbitrary", "arbitrary"), 56),
        name="proj_" + act,
    )(h, w_in)


def _attn_kernel(lamv_ref, qt_ref, k_ref, vt_ref, zs_ref, g_ref, o_ref,
                 s_ref, smax_ref, m_ref, acc_ref, *, lambda_init):
    tq = qt_ref.shape[1]
    nh, _, _, tk = vt_ref.shape
    i = pl.program_id(1)

    comp1 = lax.broadcasted_iota(jnp.int32, (V_DIM, tq), 0) < HEAD_DIM
    qz = {}
    for hh in range(nh):
        qt = qt_ref[hh * V_DIM:(hh + 1) * V_DIM, :]
        zero = jnp.zeros_like(qt)
        qz[hh, 0] = jnp.where(comp1, qt, zero)
        qz[hh, 1] = jnp.where(comp1, zero, qt)
    ones = jnp.ones((BF16_SUBLANES, tk), BF16)
    mask = (lax.broadcasted_iota(jnp.int32, (tk, tq), 0) // CHUNK
            <= lax.broadcasted_iota(jnp.int32, (tk, tq), 1) // CHUNK)
    heads = range(nh)

    def scores(hh, c, j, masked=False):
        kj = k_ref[pl.ds(pl.multiple_of(j * tk, tk), tk), hh * V_DIM:(hh + 1) * V_DIM]
        s = _dot(kj, qz[hh, c])
        if masked:
            s = jnp.where(mask, s, -jnp.inf)
        s_ref[2 * hh + c] = s
        smax_ref[2 * hh + c] = jnp.max(s, axis=0, keepdims=True)

    def consume(hh, c, j, remask=False):
        ci = 2 * hh + c
        s = s_ref[ci]
        smax = smax_ref[ci]
        if remask:
            s = jnp.where(mask, s, -jnp.inf)
            smax = jnp.max(s, axis=0, keepdims=True)
        m_old = m_ref[ci]
        m_new = jnp.maximum(m_old, smax)
        m_ref[ci] = m_new
        p = jnp.exp2(s - m_new).astype(BF16)
        vj = jnp.concatenate([vt_ref[hh, j], ones], axis=0)
        acc_ref[ci] = acc_ref[ci] * jnp.exp2(m_old - m_new) + _dot(vj, p)

    def full_tile(t):
        for hh in heads:
            scores(hh, 1, t)
        for hh in heads:
            consume(hh, 0, t)
        for hh in heads:
            scores(hh, 0, t + 1)
        for hh in heads:
            consume(hh, 1, t)

    m_ref[...] = jnp.full(m_ref.shape, -jnp.inf, F32)
    acc_ref[...] = jnp.zeros(acc_ref.shape, F32)
    for hh in heads:
        scores(hh, 0, 0)

    def four_tiles(u, carry):
        for r in range(4):
            full_tile(4 * u + r)
        return carry

    lax.fori_loop(0, jnp.right_shift(i, 2), four_tiles, 0)

    @pl.when(jnp.bitwise_and(i, 2) == 2)
    def _():
        full_tile(jnp.bitwise_and(i, -4))
        full_tile(jnp.bitwise_and(i, -4) + 1)

    @pl.when(jnp.bitwise_and(i, 1) == 1)
    def _():
        full_tile(i - 1)

    for hh in heads:
        scores(hh, 1, i, masked=True)
    for hh in heads:
        consume(hh, 0, i, remask=True)
    for hh in heads:
        consume(hh, 1, i)

    lamv = lamv_ref[...]
    lam = (jnp.exp(jnp.sum(lamv[0:1] * lamv[1:2], axis=-1, keepdims=True))
           - jnp.exp(jnp.sum(lamv[2:3] * lamv[3:4], axis=-1, keepdims=True))
           + lambda_init)
    for hh in heads:
        o1 = acc_ref[2 * hh, :V_DIM, :] / acc_ref[2 * hh, V_DIM:V_DIM + 1, :]
        o2 = acc_ref[2 * hh + 1, :V_DIM, :] / acc_ref[2 * hh + 1, V_DIM:V_DIM + 1, :]
        d = (o1 - lam * o2).T
        dn = d * lax.rsqrt(jnp.mean(d * d, axis=-1, keepdims=True) + EPS) * g_ref[...]
        cols = slice(hh * V_DIM, (hh + 1) * V_DIM)
        o_ref[:, cols] = (dn * (1.0 - lambda_init)
                          * zs_ref[:, cols].astype(F32)).astype(BF16)


def _attention(lamv, qt, k, vt, zs, subln_g, lambda_init):
    s = k.shape[0]
    assert Q_TILE == KV_TILE and Q_TILE % CHUNK == 0
    nh = ATTN_HEADS_PER_STEP
    chains = 2 * nh
    return pl.pallas_call(
        functools.partial(_attn_kernel, lambda_init=lambda_init),
        grid=(N_HEADS // nh, s // Q_TILE),
        in_specs=[
            pl.BlockSpec((4, HEAD_DIM), lambda g, i: (0, 0)),
            pl.BlockSpec((nh * V_DIM, Q_TILE), lambda g, i: (g, i)),
            pl.BlockSpec((s, nh * V_DIM), lambda g, i: (0, g)),
            pl.BlockSpec((nh, s // KV_TILE, V_DIM, KV_TILE), lambda g, i: (g, 0, 0, 0)),
            pl.BlockSpec((Q_TILE, nh * V_DIM), lambda g, i: (i, g)),
            pl.BlockSpec((1, V_DIM), lambda g, i: (0, 0)),
        ],
        out_specs=pl.BlockSpec((Q_TILE, nh * V_DIM), lambda g, i: (i, g)),
        out_shape=jax.ShapeDtypeStruct((s, D_ATTN), BF16),
        scratch_shapes=[
            pltpu.VMEM((chains, KV_TILE, Q_TILE), F32),
            pltpu.VMEM((chains, 1, Q_TILE), F32),
            pltpu.VMEM((chains, 1, Q_TILE), F32),
            pltpu.VMEM((chains, V_DIM + BF16_SUBLANES, Q_TILE), F32),
        ],
        compiler_params=_params(("parallel", "parallel"), 56),
        name="diff_attn",
    )(lamv, qt, k, vt, zs, subln_g)


def _out_kernel(yc_ref, oz_ref, ga_ref, gb_ref, wc_ref, wa_ref, wo_ref, x_ref, gate_ref,
                o_ref):
    a = _dot(yc_ref[...], wc_ref[...])
    b = _dot(oz_ref[...], wa_ref[...])
    merged = (ga_ref[...].astype(F32) * a + gb_ref[...].astype(F32) * b).astype(BF16)
    o_ref[...] = x_ref[...] + gate_ref[...] * _dot(merged, wo_ref[...])


def _merge_out(yc, oz, gates, wc, wa, wo, layer, x, gate):
    s, d = x.shape
    tm = ROW_TILE
    resident = lambda rows: pl.BlockSpec((None, rows, d), lambda m: (layer, 0, 0),
                                         pipeline_mode=pl.Buffered(1))
    return pl.pallas_call(
        _out_kernel,
        grid=(s // tm,),
        in_specs=[
            pl.BlockSpec((tm, D_CONV), lambda m: (m, 0)),
            pl.BlockSpec((tm, D_ATTN), lambda m: (m, 0)),
            pl.BlockSpec((tm, d), lambda m: (m, 0)),
            pl.BlockSpec((tm, d), lambda m: (m, 1)),
            resident(D_CONV), resident(D_ATTN), resident(d),
            pl.BlockSpec((tm, d), lambda m: (m, 0)),
            pl.BlockSpec((1, d), lambda m: (0, 0)),
        ],
        out_specs=pl.BlockSpec((tm, d), lambda m: (m, 0)),
        out_shape=jax.ShapeDtypeStruct((s, d), F32),
        compiler_params=_params(("parallel",), 56),
        name="merge_out",
    )(yc, oz, gates, gates, wc, wa, wo, x, gate)


def kernel(x, c, ada_w, ada_b, norm_g, w_in, conv_w, w_conv_out, q_norm_g, k_norm_g,
           lam_q1, lam_k1, lam_q2, lam_k2, subln_g, w_attn_out, w_o):
    bsz, seq, d = x.shape
    depth = w_in.shape[0]
    assert bsz == 1 and d == D_MODEL and seq % WIDE_ROW_TILE == 0 and seq % Q_TILE == 0
    xs = x[0]
    mod = _modulation(c, ada_w, ada_b)
    cos, sin = _rope_tables(seq)
    wc_b = w_conv_out.astype(BF16)
    wa_b = w_attn_out.astype(BF16)
    wo_b = w_o.astype(BF16)
    lamv = jnp.stack([lam_q1, lam_k1, lam_q2, lam_k2], axis=1).astype(F32)
    for l in range(depth):
        shift, scale, gate = (mod[l, :, i * d:(i + 1) * d] for i in range(3))
        h = _norm_mod(xs, norm_g[l][None, :], scale, shift)
        yc = _conv_branch(h, w_in, l, conv_w[l])
        qt, k = _qk_branch(h, w_in, l, q_norm_g[l], k_norm_g[l], cos, sin)
        vt = _v_branch(h, w_in, l)
        zs = _act_branch(h, w_in, l, OFF_ZB, D_ATTN, "silu")
        gates = _act_branch(h, w_in, l, OFF_G, 2 * d, "sigmoid")
        lambda_init = 0.8 - 0.6 * math.exp(-0.3 * l)
        oz = _attention(lamv[l], qt, k, vt, zs, subln_g[l][None, :], lambda_init)
        xs = _merge_out(yc, oz, gates, wc_b, wa_b, wo_b, l, xs, gate)
    return xs[None]
```

```python
import functools
import math

import jax
import jax.numpy as jnp
from jax import lax
from jax.experimental import pallas as pl
from jax.experimental.pallas import tpu as pltpu

F32 = jnp.float32
BF16 = jnp.bfloat16

D_MODEL = 2048
D_CONV = 1024
N_HEADS = 8
HEAD_DIM = 64
V_DIM = 2 * HEAD_DIM
D_ATTN = N_HEADS * V_DIM
CHUNK = 64
ROPE_THETA = 10000.0
EPS = 1e-6
LOG2_E = math.log2(math.e)
OFF_U, OFF_BG, OFF_CG, OFF_ZA = 0, D_CONV, 2 * D_CONV, 3 * D_CONV
OFF_Q = 4 * D_CONV
OFF_K = OFF_Q + D_ATTN
OFF_V = OFF_K + D_ATTN
OFF_ZB = OFF_V + D_ATTN
OFF_G = OFF_ZB + D_ATTN

LANES = 128
F32_SUBLANES = 8
BF16_SUBLANES = 16
MIB = 1024 * 1024

ROW_TILE = 512
WIDE_ROW_TILE = 1024
COL_TILE = 512
Q_TILE = 512
KV_TILE = 512
ATTN_HEADS_PER_STEP = 2
MOD_COL_TILE = 768


def _params(semantics, vmem_mib):
    return pltpu.CompilerParams(dimension_semantics=semantics,
                                vmem_limit_bytes=vmem_mib * MIB)


def _dot(a, b):
    return jnp.dot(a, b, preferred_element_type=F32)


def _wspec(layer, rows, tn, off):
    return pl.BlockSpec((None, rows, tn), lambda n, m: (layer, 0, off // tn + n))


def _cast_weights_once(pairs):
    @pl.when(pl.program_id(1) == 0)
    def _():
        for w_ref, wb_ref in pairs:
            wb_ref[...] = w_ref[...].astype(BF16)


def _mod_kernel(c_ref, w_ref, b_ref, o_ref):
    c = c_ref[...]
    o_ref[0] = _dot(c * jax.nn.sigmoid(c), w_ref[0]) + b_ref[0]


def _modulation(c, ada_w, ada_b):
    depth, d, n = ada_w.shape
    c8 = jnp.broadcast_to(c, (8, d))
    out = pl.pallas_call(
        _mod_kernel,
        grid=(depth, n // MOD_COL_TILE),
        in_specs=[
            pl.BlockSpec((8, d), lambda l, j: (0, 0)),
            pl.BlockSpec((1, d, MOD_COL_TILE), lambda l, j: (l, 0, j)),
            pl.BlockSpec((1, 1, MOD_COL_TILE), lambda l, j: (l, 0, j)),
        ],
        out_specs=pl.BlockSpec((1, 8, MOD_COL_TILE), lambda l, j: (l, 0, j)),
        out_shape=jax.ShapeDtypeStruct((depth, 8, n), F32),
        compiler_params=_params(("parallel", "parallel"), 32),
        name="adaln_mod",
    )(c8, ada_w, ada_b.reshape(depth, 1, n))
    return out[:, 0:1, :]


def _norm_kernel(x_ref, g_ref, scale_ref, shift_ref, h_ref):
    x = x_ref[...]
    y = x * lax.rsqrt(jnp.mean(x * x, axis=-1, keepdims=True) + EPS) * g_ref[...]
    h_ref[...] = (y * (1.0 + scale_ref[...]) + shift_ref[...]).astype(BF16)


def _norm_mod(x, g, scale, shift):
    s, d = x.shape
    row = pl.BlockSpec((1, d), lambda m: (0, 0))
    return pl.pallas_call(
        _norm_kernel,
        grid=(s // ROW_TILE,),
        in_specs=[pl.BlockSpec((ROW_TILE, d), lambda m: (m, 0)), row, row, row],
        out_specs=pl.BlockSpec((ROW_TILE, d), lambda m: (m, 0)),
        out_shape=jax.ShapeDtypeStruct((s, d), BF16),
        compiler_params=_params(("parallel",), 32),
        name="norm_mod",
    )(x, g, scale, shift)


def _conv_kernel(h_ref, wu_ref, wb_ref, wc_ref, wz_ref, cw_ref, o_ref,
                 wu_b, wb_b, wc_b, wz_b, tail_ref):
    _cast_weights_once([(wu_ref, wu_b), (wb_ref, wb_b), (wc_ref, wc_b), (wz_ref, wz_b)])
    h = h_ref[...]
    v = _dot(h, wc_b[...]) * _dot(h, wu_b[...])

    @pl.when(pl.program_id(1) == 0)
    def _():
        tail_ref[...] = jnp.zeros(tail_ref.shape, F32)

    tail = tail_ref[...]
    tail_ref[...] = v[v.shape[0] - F32_SUBLANES:, :]
    prev1 = tail[F32_SUBLANES - 1:F32_SUBLANES, :]
    prev2 = tail[F32_SUBLANES - 2:F32_SUBLANES - 1, :]
    row = lax.broadcasted_iota(jnp.int32, v.shape, 0)
    v1 = jnp.where(row == 0, prev1, pltpu.roll(v, 1, axis=0))
    v2 = jnp.where(row == 0, prev2,
                   jnp.where(row == 1, prev1, pltpu.roll(v, 2, axis=0)))
    cw = cw_ref[...]
    y = cw[0:1, :] * v2 + cw[1:2, :] * v1 + cw[2:3, :] * v
    za = _dot(h, wz_b[...])
    o_ref[...] = (_dot(h, wb_b[...]) * y * (za * jax.nn.sigmoid(za))).astype(BF16)


def _conv_branch(h, w_in, layer, conv_w):
    s, d = h.shape
    tm, tn = ROW_TILE, COL_TILE
    return pl.pallas_call(
        _conv_kernel,
        grid=(D_CONV // tn, s // tm),
        in_specs=[
            pl.BlockSpec((tm, d), lambda n, m: (m, 0)),
            _wspec(layer, d, tn, OFF_U), _wspec(layer, d, tn, OFF_BG),
            _wspec(layer, d, tn, OFF_CG), _wspec(layer, d, tn, OFF_ZA),
            pl.BlockSpec((3, tn), lambda n, m: (0, n)),
        ],
        out_specs=pl.BlockSpec((tm, tn), lambda n, m: (m, n)),
        out_shape=jax.ShapeDtypeStruct((s, D_CONV), BF16),
        scratch_shapes=[pltpu.VMEM((d, tn), BF16)] * 4
                       + [pltpu.VMEM((F32_SUBLANES, tn), F32)],
        compiler_params=_params(("arbitrary", "arbitrary"), 60),
        name="conv_branch",
    )(h, w_in, w_in, w_in, w_in, conv_w)


def _group_norm_rope(r, gsum, g, cos, sin):
    ss = _dot((r * r).astype(BF16), gsum)
    rn = r * lax.rsqrt(ss * (1.0 / HEAD_DIM) + EPS) * g
    half = HEAD_DIM // 2
    lane = lax.broadcasted_iota(jnp.int32, (r.shape[0], LANES), 1)
    first_half = (lane % HEAD_DIM) < half
    parts = []
    for a in range(r.shape[1] // LANES):
        x = rn[:, a * LANES:(a + 1) * LANES]
        partner = jnp.where(first_half, pltpu.roll(x, LANES - half, axis=1),
                            pltpu.roll(x, half, axis=1))
        parts.append(x * cos + partner * sin)
    return jnp.concatenate(parts, axis=1)


def _qk_kernel(h_ref, wq_ref, wk_ref, gsum_ref, gq_ref, gk_ref, cos_ref, sin_ref,
               qt_ref, k_ref, wq_b, wk_b):
    _cast_weights_once([(wq_ref, wq_b), (wk_ref, wk_b)])
    h = h_ref[...]
    gsum = gsum_ref[...]
    cos = cos_ref[...]
    sin = sin_ref[...]
    q = _group_norm_rope(_dot(h, wq_b[...]), gsum, gq_ref[...], cos, sin)
    qt_ref[...] = (q * (HEAD_DIM ** -0.5 * LOG2_E)).T.astype(BF16)
    k = _group_norm_rope(_dot(h, wk_b[...]), gsum, gk_ref[...], cos, sin)
    k_ref[...] = k.astype(BF16)


def _rope_tables(s):
    half = HEAD_DIM // 2
    inv = ROPE_THETA ** (-jnp.arange(half, dtype=F32) / half)
    ang = jnp.arange(s).astype(F32)[:, None] * inv[None, :]
    reps = LANES // half
    cos = jnp.tile(jnp.cos(ang), (1, reps))
    sign = jnp.where((jnp.arange(LANES) % HEAD_DIM) < half, -1.0, 1.0).astype(F32)
    sin = jnp.tile(jnp.sin(ang), (1, reps)) * sign[None, :]
    return cos, sin


def _qk_branch(h, w_in, layer, gq, gk, cos, sin):
    s, d = h.shape
    tm, tn = Q_TILE, COL_TILE
    grp = jnp.arange(tn) // HEAD_DIM
    gsum = (grp[:, None] == grp[None, :]).astype(BF16)
    gq_t = jnp.tile(gq, tn // HEAD_DIM)[None, :]
    gk_t = jnp.tile(gk, tn // HEAD_DIM)[None, :]
    const = lambda shape: pl.BlockSpec(shape, lambda n, m: (0, 0))
    return pl.pallas_call(
        _qk_kernel,
        grid=(D_ATTN // tn, s // tm),
        in_specs=[
            pl.BlockSpec((tm, d), lambda n, m: (m, 0)),
            _wspec(layer, d, tn, OFF_Q), _wspec(layer, d, tn, OFF_K),
            const((tn, tn)), const((1, tn)), const((1, tn)),
            pl.BlockSpec((tm, LANES), lambda n, m: (m, 0)),
            pl.BlockSpec((tm, LANES), lambda n, m: (m, 0)),
        ],
        out_specs=[
            pl.BlockSpec((None, tn, tm), lambda n, m: (m, n, 0)),
            pl.BlockSpec((tm, tn), lambda n, m: (m, n)),
        ],
        out_shape=[
            jax.ShapeDtypeStruct((s // tm, D_ATTN, tm), BF16),
            jax.ShapeDtypeStruct((s, D_ATTN), BF16),
        ],
        scratch_shapes=[pltpu.VMEM((d, tn), BF16)] * 2,
        compiler_params=_params(("arbitrary", "arbitrary"), 56),
        name="qk_branch",
    )(h, w_in, w_in, gsum, gq_t, gk_t, cos, sin)


def _vt_kernel(h_ref, w_ref, o_ref, w_b):
    _cast_weights_once([(w_ref, w_b)])
    vt = _dot(h_ref[...], w_b[...]).astype(BF16).T
    for hh in range(o_ref.shape[0]):
        for t in range(o_ref.shape[1]):
            o_ref[hh, t] = vt[hh * V_DIM:(hh + 1) * V_DIM, t * KV_TILE:(t + 1) * KV_TILE]


def _v_branch(h, w_in, layer):
    s, d = h.shape
    tm, tn = WIDE_ROW_TILE, COL_TILE
    return pl.pallas_call(
        _vt_kernel,
        grid=(D_ATTN // tn, s // tm),
        in_specs=[pl.BlockSpec((tm, d), lambda n, m: (m, 0)), _wspec(layer, d, tn, OFF_V)],
        out_specs=pl.BlockSpec((tn // V_DIM, tm // KV_TILE, V_DIM, KV_TILE),
                               lambda n, m: (n, m, 0, 0)),
        out_shape=jax.ShapeDtypeStruct((N_HEADS, s // KV_TILE, V_DIM, KV_TILE), BF16),
        scratch_shapes=[pltpu.VMEM((d, tn), BF16)],
        compiler_params=_params(("arbitrary", "arbitrary"), 56),
        name="v_branch",
    )(h, w_in)


def _act_kernel(h_ref, w_ref, o_ref, w_b, *, act):
    _cast_weights_once([(w_ref, w_b)])
    r = _dot(h_ref[...], w_b[...])
    sg = jax.nn.sigmoid(r)
    o_ref[...] = (r * sg if act == "silu" else sg).astype(BF16)


def _act_branch(h, w_in, layer, off, width, act):
    s, d = h.shape
    tm, tn = WIDE_ROW_TILE, 2 * COL_TILE
    return pl.pallas_call(
        functools.partial(_act_kernel, act=act),
        grid=(width // tn, s // tm),
        in_specs=[pl.BlockSpec((tm, d), lambda n, m: (m, 0)), _wspec(layer, d, tn, off)],
        out_specs=pl.BlockSpec((tm, tn), lambda n, m: (m, n)),
        out_shape=jax.ShapeDtypeStruct((s, width), BF16),
        scratch_shapes=[pltpu.VMEM((d, tn), BF16)],
        compiler_params=_params(("arbitrary", "arbitrary"), 56),
        name="proj_" + act,
    )(h, w_in)


def _attn_kernel(lamv_ref, qt_ref, k_ref, vt_ref, zs_ref, g_ref, o_ref,
                 s_ref, smax_ref, m_ref, acc_ref, *, lambda_init):
    n_tiles, _, tq = qt_ref.shape
    nh, _, _, tk = vt_ref.shape
    heads = range(nh)

    comp1 = lax.broadcasted_iota(jnp.int32, (V_DIM, tq), 0) < HEAD_DIM
    ones = jnp.ones((BF16_SUBLANES, tk), BF16)
    mask = (lax.broadcasted_iota(jnp.int32, (tk, tq), 0) // CHUNK
            <= lax.broadcasted_iota(jnp.int32, (tk, tq), 1) // CHUNK)
    lamv = lamv_ref[...]
    lam = (jnp.exp(jnp.sum(lamv[0:1] * lamv[1:2], axis=-1, keepdims=True))
           - jnp.exp(jnp.sum(lamv[2:3] * lamv[3:4], axis=-1, keepdims=True))
           + lambda_init)

    def masked_queries(i, hh, c):
        qt = qt_ref[i, hh * V_DIM:(hh + 1) * V_DIM, :]
        zero = jnp.zeros_like(qt)
        return jnp.where(comp1, qt, zero) if c == 0 else jnp.where(comp1, zero, qt)

    def scores(qz, hh, c, j, masked=False):
        kj = k_ref[pl.ds(pl.multiple_of(j * tk, tk), tk), hh * V_DIM:(hh + 1) * V_DIM]
        s = _dot(kj, qz)
        if masked:
            s = jnp.where(mask, s, -jnp.inf)
        s_ref[2 * hh + c] = s
        smax_ref[2 * hh + c] = jnp.max(s, axis=0, keepdims=True)

    def consume(hh, c, j, remask=False):
        ci = 2 * hh + c
        s = s_ref[ci]
        smax = smax_ref[ci]
        if remask:
            s = jnp.where(mask, s, -jnp.inf)
            smax = jnp.max(s, axis=0, keepdims=True)
        m_old = m_ref[ci]
        m_new = jnp.maximum(m_old, smax)
        m_ref[ci] = m_new
        p = jnp.exp2(s - m_new).astype(BF16)
        vj = jnp.concatenate([vt_ref[hh, j], ones], axis=0)
        acc_ref[ci] = acc_ref[ci] * jnp.exp2(m_old - m_new) + _dot(vj, p)

    def query_tile(i, carry):
        qz = {(hh, c): masked_queries(i, hh, c) for hh in heads for c in range(2)}

        def full_tile(t):
            for hh in heads:
                scores(qz[hh, 1], hh, 1, t)
            for hh in heads:
                consume(hh, 0, t)
            for hh in heads:
                scores(qz[hh, 0], hh, 0, t + 1)
            for hh in heads:
                consume(hh, 1, t)

        m_ref[...] = jnp.full(m_ref.shape, -jnp.inf, F32)
        acc_ref[...] = jnp.zeros(acc_ref.shape, F32)

        def four_tiles(u, c):
            for r in range(4):
                full_tile(4 * u + r)
            return c

        lax.fori_loop(0, jnp.right_shift(i, 2), four_tiles, 0)

        @pl.when(jnp.bitwise_and(i, 2) == 2)
        def _():
            full_tile(jnp.bitwise_and(i, -4))
            full_tile(jnp.bitwise_and(i, -4) + 1)

        @pl.when(jnp.bitwise_and(i, 1) == 1)
        def _():
            full_tile(i - 1)

        for hh in heads:
            scores(qz[hh, 1], hh, 1, i, masked=True)
        for hh in heads:
            consume(hh, 0, i, remask=True)
        nxt = jnp.minimum(i + 1, n_tiles - 1)
        for hh in heads:
            scores(masked_queries(nxt, hh, 0), hh, 0, 0)
        for hh in heads:
            consume(hh, 1, i)

        rows = pl.ds(pl.multiple_of(i * tq, tq), tq)
        for hh in heads:
            o1 = acc_ref[2 * hh, :V_DIM, :] / acc_ref[2 * hh, V_DIM:V_DIM + 1, :]
            o2 = acc_ref[2 * hh + 1, :V_DIM, :] / acc_ref[2 * hh + 1, V_DIM:V_DIM + 1, :]
            d = (o1 - lam * o2).T
            dn = d * lax.rsqrt(jnp.mean(d * d, axis=-1, keepdims=True) + EPS) * g_ref[...]
            cols = slice(hh * V_DIM, (hh + 1) * V_DIM)
            o_ref[rows, cols] = (dn * (1.0 - lambda_init)
                                 * zs_ref[rows, cols].astype(F32)).astype(BF16)
        return carry

    for hh in heads:
        scores(masked_queries(0, hh, 0), hh, 0, 0)
    lax.fori_loop(0, n_tiles, query_tile, 0)


def _attention(lamv, qt, k, vt, zs, subln_g, lambda_init):
    s = k.shape[0]
    assert Q_TILE == KV_TILE and Q_TILE % CHUNK == 0
    nh = ATTN_HEADS_PER_STEP
    chains = 2 * nh
    return pl.pallas_call(
        functools.partial(_attn_kernel, lambda_init=lambda_init),
        grid=(N_HEADS // nh,),
        in_specs=[
            pl.BlockSpec((4, HEAD_DIM), lambda g: (0, 0)),
            pl.BlockSpec((s // Q_TILE, nh * V_DIM, Q_TILE), lambda g: (0, g, 0)),
            pl.BlockSpec((s, nh * V_DIM), lambda g: (0, g)),
            pl.BlockSpec((nh, s // KV_TILE, V_DIM, KV_TILE), lambda g: (g, 0, 0, 0)),
            pl.BlockSpec((s, nh * V_DIM), lambda g: (0, g)),
            pl.BlockSpec((1, V_DIM), lambda g: (0, 0)),
        ],
        out_specs=pl.BlockSpec((s, nh * V_DIM), lambda g: (0, g)),
        out_shape=jax.ShapeDtypeStruct((s, D_ATTN), BF16),
        scratch_shapes=[
            pltpu.VMEM((chains, KV_TILE, Q_TILE), F32),
            pltpu.VMEM((chains, 1, Q_TILE), F32),
            pltpu.VMEM((chains, 1, Q_TILE), F32),
            pltpu.VMEM((chains, V_DIM + BF16_SUBLANES, Q_TILE), F32),
        ],
        compiler_params=_params(("parallel",), 60),
        name="diff_attn",
    )(lamv, qt, k, vt, zs, subln_g)


def _out_kernel(yc_ref, oz_ref, ga_ref, gb_ref, wc_ref, wa_ref, wo_ref, x_ref, gate_ref,
                o_ref):
    a = _dot(yc_ref[...], wc_ref[...])
    b = _dot(oz_ref[...], wa_ref[...])
    merged = (ga_ref[...].astype(F32) * a + gb_ref[...].astype(F32) * b).astype(BF16)
    o_ref[...] = x_ref[...] + gate_ref[...] * _dot(merged, wo_ref[...])


def _merge_out(yc, oz, gates, wc, wa, wo, layer, x, gate):
    s, d = x.shape
    tm = ROW_TILE
    resident = lambda rows: pl.BlockSpec((None, rows, d), lambda m: (layer, 0, 0),
                                         pipeline_mode=pl.Buffered(1))
    return pl.pallas_call(
        _out_kernel,
        grid=(s // tm,),
        in_specs=[
            pl.BlockSpec((tm, D_CONV), lambda m: (m, 0)),
            pl.BlockSpec((tm, D_ATTN), lambda m: (m, 0)),
            pl.BlockSpec((tm, d), lambda m: (m, 0)),
            pl.BlockSpec((tm, d), lambda m: (m, 1)),
            resident(D_CONV), resident(D_ATTN), resident(d),
            pl.BlockSpec((tm, d), lambda m: (m, 0)),
            pl.BlockSpec((1, d), lambda m: (0, 0)),
        ],
        out_specs=pl.BlockSpec((tm, d), lambda m: (m, 0)),
        out_shape=jax.ShapeDtypeStruct((s, d), F32),
        compiler_params=_params(("parallel",), 56),
        name="merge_out",
    )(yc, oz, gates, gates, wc, wa, wo, x, gate)


def kernel(x, c, ada_w, ada_b, norm_g, w_in, conv_w, w_conv_out, q_norm_g, k_norm_g,
           lam_q1, lam_k1, lam_q2, lam_k2, subln_g, w_attn_out, w_o):
    bsz, seq, d = x.shape
    depth = w_in.shape[0]
    assert bsz == 1 and d == D_MODEL and seq % WIDE_ROW_TILE == 0 and seq % Q_TILE == 0
    xs = x[0]
    mod = _modulation(c, ada_w, ada_b)
    cos, sin = _rope_tables(seq)
    wc_b = w_conv_out.astype(BF16)
    wa_b = w_attn_out.astype(BF16)
    wo_b = w_o.astype(BF16)
    lamv = jnp.stack([lam_q1, lam_k1, lam_q2, lam_k2], axis=1).astype(F32)
    for l in range(depth):
        shift, scale, gate = (mod[l, :, i * d:(i + 1) * d] for i in range(3))
        h = _norm_mod(xs, norm_g[l][None, :], scale, shift)
        yc = _conv_branch(h, w_in, l, conv_w[l])
        qt, k = _qk_branch(h, w_in, l, q_norm_g[l], k_norm_g[l], cos, sin)
        vt = _v_branch(h, w_in, l)
        zs = _act_branch(h, w_in, l, OFF_ZB, D_ATTN, "silu")
        gates = _act_branch(h, w_in, l, OFF_G, 2 * d, "sigmoid")
        lambda_init = 0.8 - 0.6 * math.exp(-0.3 * l)
        oz = _attention(lamv[l], qt, k, vt, zs, subln_g[l][None, :], lambda_init)
        xs = _merge_out(yc, oz, gates, wc_b, wa_b, wo_b, l, xs, gate)
    return xs[None]
```

```python
import functools
import math

import jax
import jax.numpy as jnp
from jax import lax
from jax.experimental import pallas as pl
from jax.experimental.pallas import tpu as pltpu

F32 = jnp.float32
BF16 = jnp.bfloat16

D_MODEL = 2048
D_CONV = 1024
N_HEADS = 8
HEAD_DIM = 64
V_DIM = 2 * HEAD_DIM
D_ATTN = N_HEADS * V_DIM
CHUNK = 64
ROPE_THETA = 10000.0
EPS = 1e-6
LOG2_E = math.log2(math.e)
OFF_U, OFF_BG, OFF_CG, OFF_ZA = 0, D_CONV, 2 * D_CONV, 3 * D_CONV
OFF_Q = 4 * D_CONV
OFF_K = OFF_Q + D_ATTN
OFF_V = OFF_K + D_ATTN
OFF_ZB = OFF_V + D_ATTN
OFF_G = OFF_ZB + D_ATTN

LANES = 128
F32_SUBLANES = 8
BF16_SUBLANES = 16
MIB = 1024 * 1024

ROW_TILE = 512
WIDE_ROW_TILE = 1024
COL_TILE = 512
Q_TILE = 512
KV_TILE = 512
ATTN_HEADS_PER_STEP = 2
MOD_COL_TILE = 768


def _params(semantics, vmem_mib):
    return pltpu.CompilerParams(dimension_semantics=semantics,
                                vmem_limit_bytes=vmem_mib * MIB)


def _dot(a, b):
    return jnp.dot(a, b, preferred_element_type=F32)


def _wspec(layer, rows, tn, off):
    return pl.BlockSpec((None, rows, tn), lambda n, m: (layer, 0, off // tn + n))


def _cast_weights_once(pairs):
    @pl.when(pl.program_id(1) == 0)
    def _():
        for w_ref, wb_ref in pairs:
            wb_ref[...] = w_ref[...].astype(BF16)


def _mod_kernel(c_ref, w_ref, b_ref, o_ref):
    c = c_ref[...]
    o_ref[0] = _dot(c * jax.nn.sigmoid(c), w_ref[0]) + b_ref[0]


def _modulation(c, ada_w, ada_b):
    depth, d, n = ada_w.shape
    c8 = jnp.broadcast_to(c, (8, d))
    out = pl.pallas_call(
        _mod_kernel,
        grid=(depth, n // MOD_COL_TILE),
        in_specs=[
            pl.BlockSpec((8, d), lambda l, j: (0, 0)),
            pl.BlockSpec((1, d, MOD_COL_TILE), lambda l, j: (l, 0, j)),
            pl.BlockSpec((1, 1, MOD_COL_TILE), lambda l, j: (l, 0, j)),
        ],
        out_specs=pl.BlockSpec((1, 8, MOD_COL_TILE), lambda l, j: (l, 0, j)),
        out_shape=jax.ShapeDtypeStruct((depth, 8, n), F32),
        compiler_params=_params(("parallel", "parallel"), 32),
        name="adaln_mod",
    )(c8, ada_w, ada_b.reshape(depth, 1, n))
    return out[:, 0:1, :]


def _norm_kernel(x_ref, g_ref, scale_ref, shift_ref, h_ref):
    x = x_ref[...]
    y = x * lax.rsqrt(jnp.mean(x * x, axis=-1, keepdims=True) + EPS) * g_ref[...]
    h_ref[...] = (y * (1.0 + scale_ref[...]) + shift_ref[...]).astype(BF16)


def _norm_mod(x, g, scale, shift):
    s, d = x.shape
    row = pl.BlockSpec((1, d), lambda m: (0, 0))
    return pl.pallas_call(
        _norm_kernel,
        grid=(s // ROW_TILE,),
        in_specs=[pl.BlockSpec((ROW_TILE, d), lambda m: (m, 0)), row, row, row],
        out_specs=pl.BlockSpec((ROW_TILE, d), lambda m: (m, 0)),
        out_shape=jax.ShapeDtypeStruct((s, d), BF16),
        compiler_params=_params(("parallel",), 32),
        name="norm_mod",
    )(x, g, scale, shift)


def _conv_kernel(h_ref, wu_ref, wb_ref, wc_ref, wz_ref, cw_ref, o_ref,
                 wu_b, wb_b, wc_b, wz_b, tail_ref):
    _cast_weights_once([(wu_ref, wu_b), (wb_ref, wb_b), (wc_ref, wc_b), (wz_ref, wz_b)])
    h = h_ref[...]
    v = _dot(h, wc_b[...]) * _dot(h, wu_b[...])

    @pl.when(pl.program_id(1) == 0)
    def _():
        tail_ref[...] = jnp.zeros(tail_ref.shape, F32)

    tail = tail_ref[...]
    tail_ref[...] = v[v.shape[0] - F32_SUBLANES:, :]
    prev1 = tail[F32_SUBLANES - 1:F32_SUBLANES, :]
    prev2 = tail[F32_SUBLANES - 2:F32_SUBLANES - 1, :]
    row = lax.broadcasted_iota(jnp.int32, v.shape, 0)
    v1 = jnp.where(row == 0, prev1, pltpu.roll(v, 1, axis=0))
    v2 = jnp.where(row == 0, prev2,
                   jnp.where(row == 1, prev1, pltpu.roll(v, 2, axis=0)))
    cw = cw_ref[...]
    y = cw[0:1, :] * v2 + cw[1:2, :] * v1 + cw[2:3, :] * v
    za = _dot(h, wz_b[...])
    o_ref[...] = (_dot(h, wb_b[...]) * y * (za * jax.nn.sigmoid(za))).astype(BF16)


def _conv_branch(h, w_in, layer, conv_w):
    s, d = h.shape
    tm, tn = ROW_TILE, COL_TILE
    return pl.pallas_call(
        _conv_kernel,
        grid=(D_CONV // tn, s // tm),
        in_specs=[
            pl.BlockSpec((tm, d), lambda n, m: (m, 0)),
            _wspec(layer, d, tn, OFF_U), _wspec(layer, d, tn, OFF_BG),
            _wspec(layer, d, tn, OFF_CG), _wspec(layer, d, tn, OFF_ZA),
            pl.BlockSpec((3, tn), lambda n, m: (0, n)),
        ],
        out_specs=pl.BlockSpec((tm, tn), lambda n, m: (m, n)),
        out_shape=jax.ShapeDtypeStruct((s, D_CONV), BF16),
        scratch_shapes=[pltpu.VMEM((d, tn), BF16)] * 4
                       + [pltpu.VMEM((F32_SUBLANES, tn), F32)],
        compiler_params=_params(("arbitrary", "arbitrary"), 60),
        name="conv_branch",
    )(h, w_in, w_in, w_in, w_in, conv_w)


def _group_norm_rope(r, gsum, g, cos, sin):
    ss = _dot((r * r).astype(BF16), gsum)
    rn = r * lax.rsqrt(ss * (1.0 / HEAD_DIM) + EPS) * g
    half = HEAD_DIM // 2
    lane = lax.broadcasted_iota(jnp.int32, (r.shape[0], LANES), 1)
    first_half = (lane % HEAD_DIM) < half
    parts = []
    for a in range(r.shape[1] // LANES):
        x = rn[:, a * LANES:(a + 1) * LANES]
        partner = jnp.where(first_half, pltpu.roll(x, LANES - half, axis=1),
                            pltpu.roll(x, half, axis=1))
        parts.append(x * cos + partner * sin)
    return jnp.concatenate(parts, axis=1)


def _qk_kernel(h_ref, wq_ref, wk_ref, gsum_ref, gq_ref, gk_ref, cos_ref, sin_ref,
               qt_ref, k_ref, wq_b, wk_b):
    _cast_weights_once([(wq_ref, wq_b), (wk_ref, wk_b)])
    h = h_ref[...]
    gsum = gsum_ref[...]
    cos = cos_ref[...]
    sin = sin_ref[...]
    q = _group_norm_rope(_dot(h, wq_b[...]), gsum, gq_ref[...], cos, sin)
    qt_ref[...] = (q * (HEAD_DIM ** -0.5 * LOG2_E)).T.astype(BF16)
    k = _group_norm_rope(_dot(h, wk_b[...]), gsum, gk_ref[...], cos, sin)
    k_ref[...] = k.astype(BF16)


def _rope_tables(s):
    half = HEAD_DIM // 2
    inv = ROPE_THETA ** (-jnp.arange(half, dtype=F32) / half)
    ang = jnp.arange(s).astype(F32)[:, None] * inv[None, :]
    reps = LANES // half
    cos = jnp.tile(jnp.cos(ang), (1, reps))
    sign = jnp.where((jnp.arange(LANES) % HEAD_DIM) < half, -1.0, 1.0).astype(F32)
    sin = jnp.tile(jnp.sin(ang), (1, reps)) * sign[None, :]
    return cos, sin


def _qk_branch(h, w_in, layer, gq, gk, cos, sin):
    s, d = h.shape
    tm, tn = Q_TILE, COL_TILE
    grp = jnp.arange(tn) // HEAD_DIM
    gsum = (grp[:, None] == grp[None, :]).astype(BF16)
    gq_t = jnp.tile(gq, tn // HEAD_DIM)[None, :]
    gk_t = jnp.tile(gk, tn // HEAD_DIM)[None, :]
    const = lambda shape: pl.BlockSpec(shape, lambda n, m: (0, 0))
    return pl.pallas_call(
        _qk_kernel,
        grid=(D_ATTN // tn, s // tm),
        in_specs=[
            pl.BlockSpec((tm, d), lambda n, m: (m, 0)),
            _wspec(layer, d, tn, OFF_Q), _wspec(layer, d, tn, OFF_K),
            const((tn, tn)), const((1, tn)), const((1, tn)),
            pl.BlockSpec((tm, LANES), lambda n, m: (m, 0)),
            pl.BlockSpec((tm, LANES), lambda n, m: (m, 0)),
        ],
        out_specs=[
            pl.BlockSpec((None, tn, tm), lambda n, m: (m, n, 0)),
            pl.BlockSpec((tm, tn), lambda n, m: (m, n)),
        ],
        out_shape=[
            jax.ShapeDtypeStruct((s // tm, D_ATTN, tm), BF16),
            jax.ShapeDtypeStruct((s, D_ATTN), BF16),
        ],
        scratch_shapes=[pltpu.VMEM((d, tn), BF16)] * 2,
        compiler_params=_params(("arbitrary", "arbitrary"), 56),
        name="qk_branch",
    )(h, w_in, w_in, gsum, gq_t, gk_t, cos, sin)


def _vt_kernel(h_ref, w_ref, o_ref, w_b):
    _cast_weights_once([(w_ref, w_b)])
    vt = _dot(h_ref[...], w_b[...]).astype(BF16).T
    for hh in range(o_ref.shape[0]):
        for t in range(o_ref.shape[1]):
            o_ref[hh, t] = vt[hh * V_DIM:(hh + 1) * V_DIM, t * KV_TILE:(t + 1) * KV_TILE]


def _v_branch(h, w_in, layer):
    s, d = h.shape
    tm, tn = WIDE_ROW_TILE, COL_TILE
    return pl.pallas_call(
        _vt_kernel,
        grid=(D_ATTN // tn, s // tm),
        in_specs=[pl.BlockSpec((tm, d), lambda n, m: (m, 0)), _wspec(layer, d, tn, OFF_V)],
        out_specs=pl.BlockSpec((tn // V_DIM, tm // KV_TILE, V_DIM, KV_TILE),
                               lambda n, m: (n, m, 0, 0)),
        out_shape=jax.ShapeDtypeStruct((N_HEADS, s // KV_TILE, V_DIM, KV_TILE), BF16),
        scratch_shapes=[pltpu.VMEM((d, tn), BF16)],
        compiler_params=_params(("arbitrary", "arbitrary"), 56),
        name="v_branch",
    )(h, w_in)


def _act_kernel(h_ref, w_ref, o_ref, w_b, *, act):
    _cast_weights_once([(w_ref, w_b)])
    r = _dot(h_ref[...], w_b[...])
    sg = jax.nn.sigmoid(r)
    o_ref[...] = (r * sg if act == "silu" else sg).astype(BF16)


def _act_branch(h, w_in, layer, off, width, act):
    s, d = h.shape
    tm, tn = WIDE_ROW_TILE, 2 * COL_TILE
    return pl.pallas_call(
        functools.partial(_act_kernel, act=act),
        grid=(width // tn, s // tm),
        in_specs=[pl.BlockSpec((tm, d), lambda n, m: (m, 0)), _wspec(layer, d, tn, off)],
        out_specs=pl.BlockSpec((tm, tn), lambda n, m: (m, n)),
        out_shape=jax.ShapeDtypeStruct((s, width), BF16),
        scratch_shapes=[pltpu.VMEM((d, tn), BF16)],
        compiler_params=_params(("arbitrary", "arbitrary"), 56),
        name="proj_" + act,
    )(h, w_in)


def _attn_kernel(lamv_ref, qt_ref, k_ref, vt_ref, zs_ref, g_ref, o_ref,
                 s_ref, smax_ref, m_ref, acc_ref, *, lambda_init):
    n_tiles, _, tq = qt_ref.shape
    nh, _, _, tk = vt_ref.shape
    heads = range(nh)

    comp1 = lax.broadcasted_iota(jnp.int32, (V_DIM, tq), 0) < HEAD_DIM
    ones = jnp.ones((BF16_SUBLANES, tk), BF16)
    mask = (lax.broadcasted_iota(jnp.int32, (tk, tq), 0) // CHUNK
            <= lax.broadcasted_iota(jnp.int32, (tk, tq), 1) // CHUNK)
    lamv = lamv_ref[...]
    lam = (jnp.exp(jnp.sum(lamv[0:1] * lamv[1:2], axis=-1, keepdims=True))
           - jnp.exp(jnp.sum(lamv[2:3] * lamv[3:4], axis=-1, keepdims=True))
           + lambda_init)

    def masked_queries(i, hh, c):
        qt = qt_ref[i, hh * V_DIM:(hh + 1) * V_DIM, :]
        zero = jnp.zeros_like(qt)
        return jnp.where(comp1, qt, zero) if c == 0 else jnp.where(comp1, zero, qt)

    def scores(qz, hh, c, j, masked=False):
        kj = k_ref[pl.ds(pl.multiple_of(j * tk, tk), tk), hh * V_DIM:(hh + 1) * V_DIM]
        s = _dot(kj, qz)
        if masked:
            s = jnp.where(mask, s, -jnp.inf)
        s_ref[2 * hh + c] = s
        smax_ref[2 * hh + c] = jnp.max(s, axis=0, keepdims=True)

    def consume(hh, c, j, remask=False):
        ci = 2 * hh + c
        s = s_ref[ci]
        smax = smax_ref[ci]
        if remask:
            s = jnp.where(mask, s, -jnp.inf)
            smax = jnp.max(s, axis=0, keepdims=True)
        m_old = m_ref[ci]
        m_new = jnp.maximum(m_old, smax)
        m_ref[ci] = m_new
        p = jnp.exp2(s - m_new).astype(BF16)
        vj = jnp.concatenate([vt_ref[hh, j], ones], axis=0)
        acc_ref[ci] = acc_ref[ci] * jnp.exp2(m_old - m_new) + _dot(vj, p)

    def query_tile(i, carry):
        qz = {(hh, c): masked_queries(i, hh, c) for hh in heads for c in range(2)}

        def full_tile(t):
            for hh in heads:
                scores(qz[hh, 1], hh, 1, t)
            for hh in heads:
                consume(hh, 0, t)
            for hh in heads:
                scores(qz[hh, 0], hh, 0, t + 1)
            for hh in heads:
                consume(hh, 1, t)

        m_ref[...] = jnp.full(m_ref.shape, -jnp.inf, F32)
        acc_ref[...] = jnp.zeros(acc_ref.shape, F32)

        def four_tiles(u, c):
            for r in range(4):
                full_tile(4 * u + r)
            return c

        lax.fori_loop(0, jnp.right_shift(i, 2), four_tiles, 0)

        @pl.when(jnp.bitwise_and(i, 2) == 2)
        def _():
            full_tile(jnp.bitwise_and(i, -4))
            full_tile(jnp.bitwise_and(i, -4) + 1)

        @pl.when(jnp.bitwise_and(i, 1) == 1)
        def _():
            full_tile(i - 1)

        for hh in heads:
            scores(qz[hh, 1], hh, 1, i, masked=True)
        for hh in heads:
            consume(hh, 0, i, remask=True)
        nxt = jnp.minimum(i + 1, n_tiles - 1)
        for hh in heads:
            scores(masked_queries(nxt, hh, 0), hh, 0, 0)
        for hh in heads:
            consume(hh, 1, i)

        rows = pl.ds(pl.multiple_of(i * tq, tq), tq)
        for hh in heads:
            o1 = acc_ref[2 * hh, :V_DIM, :] / acc_ref[2 * hh, V_DIM:V_DIM + 1, :]
            o2 = acc_ref[2 * hh + 1, :V_DIM, :] / acc_ref[2 * hh + 1, V_DIM:V_DIM + 1, :]
            d = (o1 - lam * o2).T
            dn = d * lax.rsqrt(jnp.mean(d * d, axis=-1, keepdims=True) + EPS) * g_ref[...]
            cols = slice(hh * V_DIM, (hh + 1) * V_DIM)
            o_ref[rows, cols] = (dn * (1.0 - lambda_init)
                                 * zs_ref[rows, cols].astype(F32)).astype(BF16)
        return carry

    for hh in heads:
        scores(masked_queries(0, hh, 0), hh, 0, 0)
    lax.fori_loop(0, n_tiles, query_tile, 0)


def _attention(lamv, qt, k, vt, zs, subln_g, lambda_init):
    s = k.shape[0]
    assert Q_TILE == KV_TILE and Q_TILE % CHUNK == 0
    nh = ATTN_HEADS_PER_STEP
    chains = 2 * nh
    return pl.pallas_call(
        functools.partial(_attn_kernel, lambda_init=lambda_init),
        grid=(N_HEADS // nh,),
        in_specs=[
            pl.BlockSpec((4, HEAD_DIM), lambda g: (0, 0)),
            pl.BlockSpec((s // Q_TILE, nh * V_DIM, Q_TILE), lambda g: (0, g, 0)),
            pl.BlockSpec((s, nh * V_DIM), lambda g: (0, g)),
            pl.BlockSpec((nh, s // KV_TILE, V_DIM, KV_TILE), lambda g: (g, 0, 0, 0)),
            pl.BlockSpec((s, nh * V_DIM), lambda g: (0, g)),
            pl.BlockSpec((1, V_DIM), lambda g: (0, 0)),
        ],
        out_specs=pl.BlockSpec((s, nh * V_DIM), lambda g: (0, g)),
        out_shape=jax.ShapeDtypeStruct((s, D_ATTN), BF16),
        scratch_shapes=[
            pltpu.VMEM((chains, KV_TILE, Q_TILE), F32),
            pltpu.VMEM((chains, 1, Q_TILE), F32),
            pltpu.VMEM((chains, 1, Q_TILE), F32),
            pltpu.VMEM((chains, V_DIM + BF16_SUBLANES, Q_TILE), F32),
        ],
        compiler_params=_params(("parallel",), 60),
        name="diff_attn",
    )(lamv, qt, k, vt, zs, subln_g)


def _out_kernel(yc_ref, oz_ref, ga_ref, gb_ref, wc_ref, wa_ref, wo_ref, x_ref, gate_ref,
                *rest, emit_next_h):
    a = _dot(yc_ref[...], wc_ref[...])
    b = _dot(oz_ref[...], wa_ref[...])
    merged = (ga_ref[...].astype(F32) * a + gb_ref[...].astype(F32) * b).astype(BF16)
    x_new = x_ref[...] + gate_ref[...] * _dot(merged, wo_ref[...])
    if emit_next_h:
        g_ref, scale_ref, shift_ref, o_ref, h_ref = rest
        _norm_kernel(x_new, g_ref, scale_ref, shift_ref, h_ref)
    else:
        o_ref, = rest
    o_ref[...] = x_new


def _merge_out(yc, oz, gates, wc, wa, wo, layer, x, gate, next_norm=None):
    s, d = x.shape
    tm = ROW_TILE
    resident = lambda rows: pl.BlockSpec((None, rows, d), lambda m: (layer, 0, 0),
                                         pipeline_mode=pl.Buffered(1))
    row = pl.BlockSpec((1, d), lambda m: (0, 0))
    tile = pl.BlockSpec((tm, d), lambda m: (m, 0))
    emit = next_norm is not None
    return pl.pallas_call(
        functools.partial(_out_kernel, emit_next_h=emit),
        grid=(s // tm,),
        in_specs=[
            pl.BlockSpec((tm, D_CONV), lambda m: (m, 0)),
            pl.BlockSpec((tm, D_ATTN), lambda m: (m, 0)),
            pl.BlockSpec((tm, d), lambda m: (m, 0)),
            pl.BlockSpec((tm, d), lambda m: (m, 1)),
            resident(D_CONV), resident(D_ATTN), resident(d),
            tile, row,
        ] + ([row, row, row] if emit else []),
        out_specs=[tile, tile] if emit else tile,
        out_shape=([jax.ShapeDtypeStruct((s, d), F32), jax.ShapeDtypeStruct((s, d), BF16)]
                   if emit else jax.ShapeDtypeStruct((s, d), F32)),
        compiler_params=_params(("parallel",), 60),
        name="merge_out",
    )(yc, oz, gates, gates, wc, wa, wo, x, gate, *(next_norm if emit else ()))


def kernel(x, c, ada_w, ada_b, norm_g, w_in, conv_w, w_conv_out, q_norm_g, k_norm_g,
           lam_q1, lam_k1, lam_q2, lam_k2, subln_g, w_attn_out, w_o):
    bsz, seq, d = x.shape
    depth = w_in.shape[0]
    assert bsz == 1 and d == D_MODEL and seq % WIDE_ROW_TILE == 0 and seq % Q_TILE == 0
    xs = x[0]
    mod = _modulation(c, ada_w, ada_b)
    cos, sin = _rope_tables(seq)
    wc_b = w_conv_out.astype(BF16)
    wa_b = w_attn_out.astype(BF16)
    wo_b = w_o.astype(BF16)
    lamv = jnp.stack([lam_q1, lam_k1, lam_q2, lam_k2], axis=1).astype(F32)
    mods = [tuple(mod[l, :, i * d:(i + 1) * d] for i in range(3)) for l in range(depth)]
    h = _norm_mod(xs, norm_g[0][None, :], mods[0][1], mods[0][0])
    for l in range(depth):
        gate = mods[l][2]
        yc = _conv_branch(h, w_in, l, conv_w[l])
        qt, k = _qk_branch(h, w_in, l, q_norm_g[l], k_norm_g[l], cos, sin)
        vt = _v_branch(h, w_in, l)
        zs = _act_branch(h, w_in, l, OFF_ZB, D_ATTN, "silu")
        gates = _act_branch(h, w_in, l, OFF_G, 2 * d, "sigmoid")
        lambda_init = 0.8 - 0.6 * math.exp(-0.3 * l)
        oz = _attention(lamv[l], qt, k, vt, zs, subln_g[l][None, :], lambda_init)
        if l + 1 < depth:
            nxt = (norm_g[l + 1][None, :], mods[l + 1][1], mods[l + 1][0])
            xs, h = _merge_out(yc, oz, gates, wc_b, wa_b, wo_b, l, xs, gate, nxt)
        else:
            xs = _merge_out(yc, oz, gates, wc_b, wa_b, wo_b, l, xs, gate)
    return xs[None]
```

```python
import functools
import math

import jax
import jax.numpy as jnp
from jax import lax
from jax.experimental import pallas as pl
from jax.experimental.pallas import tpu as pltpu

F32 = jnp.float32
BF16 = jnp.bfloat16

D_MODEL = 2048
D_CONV = 1024
N_HEADS = 8
HEAD_DIM = 64
V_DIM = 2 * HEAD_DIM
D_ATTN = N_HEADS * V_DIM
CHUNK = 64
ROPE_THETA = 10000.0
EPS = 1e-6
LOG2_E = math.log2(math.e)
OFF_U, OFF_BG, OFF_CG, OFF_ZA = 0, D_CONV, 2 * D_CONV, 3 * D_CONV
OFF_Q = 4 * D_CONV
OFF_K = OFF_Q + D_ATTN
OFF_V = OFF_K + D_ATTN
OFF_ZB = OFF_V + D_ATTN
OFF_G = OFF_ZB + D_ATTN

LANES = 128
F32_SUBLANES = 8
BF16_SUBLANES = 16
MIB = 1024 * 1024

ROW_TILE = 512
WIDE_ROW_TILE = 1024
COL_TILE = 512
Q_TILE = 512
KV_TILE = 512
ATTN_HEADS_PER_STEP = 2
MOD_COL_TILE = 768


def _params(semantics, vmem_mib):
    return pltpu.CompilerParams(dimension_semantics=semantics,
                                vmem_limit_bytes=vmem_mib * MIB)


def _dot(a, b):
    return jnp.dot(a, b, preferred_element_type=F32)


def _sigmoid(x):
    return 0.5 * jnp.tanh(0.5 * x) + 0.5


def _wspec(layer, rows, tn, off):
    return pl.BlockSpec((None, rows, tn), lambda n, m: (layer, 0, off // tn + n))


def _cast_weights_once(pairs):
    @pl.when(pl.program_id(1) == 0)
    def _():
        for w_ref, wb_ref in pairs:
            wb_ref[...] = w_ref[...].astype(BF16)


def _mod_kernel(c_ref, w_ref, b_ref, o_ref):
    c = c_ref[...]
    o_ref[0] = _dot(c * jax.nn.sigmoid(c), w_ref[0]) + b_ref[0]


def _modulation(c, ada_w, ada_b):
    depth, d, n = ada_w.shape
    c8 = jnp.broadcast_to(c, (8, d))
    out = pl.pallas_call(
        _mod_kernel,
        grid=(depth, n // MOD_COL_TILE),
        in_specs=[
            pl.BlockSpec((8, d), lambda l, j: (0, 0)),
            pl.BlockSpec((1, d, MOD_COL_TILE), lambda l, j: (l, 0, j)),
            pl.BlockSpec((1, 1, MOD_COL_TILE), lambda l, j: (l, 0, j)),
        ],
        out_specs=pl.BlockSpec((1, 8, MOD_COL_TILE), lambda l, j: (l, 0, j)),
        out_shape=jax.ShapeDtypeStruct((depth, 8, n), F32),
        compiler_params=_params(("parallel", "parallel"), 32),
        name="adaln_mod",
    )(c8, ada_w, ada_b.reshape(depth, 1, n))
    return out[:, 0:1, :]


def _norm_kernel(x_ref, g_ref, scale_ref, shift_ref, h_ref):
    x = x_ref[...]
    y = x * lax.rsqrt(jnp.mean(x * x, axis=-1, keepdims=True) + EPS) * g_ref[...]
    h_ref[...] = (y * (1.0 + scale_ref[...]) + shift_ref[...]).astype(BF16)


def _norm_mod(x, g, scale, shift):
    s, d = x.shape
    row = pl.BlockSpec((1, d), lambda m: (0, 0))
    return pl.pallas_call(
        _norm_kernel,
        grid=(s // ROW_TILE,),
        in_specs=[pl.BlockSpec((ROW_TILE, d), lambda m: (m, 0)), row, row, row],
        out_specs=pl.BlockSpec((ROW_TILE, d), lambda m: (m, 0)),
        out_shape=jax.ShapeDtypeStruct((s, d), BF16),
        compiler_params=_params(("parallel",), 32),
        name="norm_mod",
    )(x, g, scale, shift)


def _conv_kernel(h_ref, wu_ref, wb_ref, wc_ref, wz_ref, cw_ref, o_ref,
                 wu_b, wb_b, wc_b, wz_b, tail_ref):
    _cast_weights_once([(wu_ref, wu_b), (wb_ref, wb_b), (wc_ref, wc_b), (wz_ref, wz_b)])
    h = h_ref[...]
    v = _dot(h, wc_b[...]) * _dot(h, wu_b[...])

    @pl.when(pl.program_id(1) == 0)
    def _():
        tail_ref[...] = jnp.zeros(tail_ref.shape, F32)

    tail = tail_ref[...]
    tail_ref[...] = v[v.shape[0] - F32_SUBLANES:, :]
    prev1 = tail[F32_SUBLANES - 1:F32_SUBLANES, :]
    prev2 = tail[F32_SUBLANES - 2:F32_SUBLANES - 1, :]
    row = lax.broadcasted_iota(jnp.int32, v.shape, 0)
    v1 = jnp.where(row == 0, prev1, pltpu.roll(v, 1, axis=0))
    v2 = jnp.where(row == 0, prev2,
                   jnp.where(row == 1, prev1, pltpu.roll(v, 2, axis=0)))
    cw = cw_ref[...]
    y = cw[0:1, :] * v2 + cw[1:2, :] * v1 + cw[2:3, :] * v
    za = _dot(h, wz_b[...])
    o_ref[...] = (_dot(h, wb_b[...]) * y * (za * _sigmoid(za))).astype(BF16)


def _conv_branch(h, w_in, layer, conv_w):
    s, d = h.shape
    tm, tn = ROW_TILE, COL_TILE
    return pl.pallas_call(
        _conv_kernel,
        grid=(D_CONV // tn, s // tm),
        in_specs=[
            pl.BlockSpec((tm, d), lambda n, m: (m, 0)),
            _wspec(layer, d, tn, OFF_U), _wspec(layer, d, tn, OFF_BG),
            _wspec(layer, d, tn, OFF_CG), _wspec(layer, d, tn, OFF_ZA),
            pl.BlockSpec((3, tn), lambda n, m: (0, n)),
        ],
        out_specs=pl.BlockSpec((tm, tn), lambda n, m: (m, n)),
        out_shape=jax.ShapeDtypeStruct((s, D_CONV), BF16),
        scratch_shapes=[pltpu.VMEM((d, tn), BF16)] * 4
                       + [pltpu.VMEM((F32_SUBLANES, tn), F32)],
        compiler_params=_params(("arbitrary", "arbitrary"), 60),
        name="conv_branch",
    )(h, w_in, w_in, w_in, w_in, conv_w)


def _group_norm_rope(r, gsum, g, cos, sin):
    ss = _dot((r * r).astype(BF16), gsum)
    rn = r * lax.rsqrt(ss * (1.0 / HEAD_DIM) + EPS) * g
    half = HEAD_DIM // 2
    lane = lax.broadcasted_iota(jnp.int32, (r.shape[0], LANES), 1)
    first_half = (lane % HEAD_DIM) < half
    parts = []
    for a in range(r.shape[1] // LANES):
        x = rn[:, a * LANES:(a + 1) * LANES]
        partner = jnp.where(first_half, pltpu.roll(x, LANES - half, axis=1),
                            pltpu.roll(x, half, axis=1))
        parts.append(x * cos + partner * sin)
    return jnp.concatenate(parts, axis=1)


def _qk_kernel(h_ref, wq_ref, wk_ref, gsum_ref, gq_ref, gk_ref, cos_ref, sin_ref,
               qt_ref, k_ref, wq_b, wk_b):
    _cast_weights_once([(wq_ref, wq_b), (wk_ref, wk_b)])
    h = h_ref[...]
    gsum = gsum_ref[...]
    cos = cos_ref[...]
    sin = sin_ref[...]
    q = _group_norm_rope(_dot(h, wq_b[...]), gsum, gq_ref[...], cos, sin)
    qt_ref[...] = (q * (HEAD_DIM ** -0.5 * LOG2_E)).T.astype(BF16)
    k = _group_norm_rope(_dot(h, wk_b[...]), gsum, gk_ref[...], cos, sin)
    k_ref[...] = k.astype(BF16)


def _rope_tables(s):
    half = HEAD_DIM // 2
    inv = ROPE_THETA ** (-jnp.arange(half, dtype=F32) / half)
    ang = jnp.arange(s).astype(F32)[:, None] * inv[None, :]
    reps = LANES // half
    cos = jnp.tile(jnp.cos(ang), (1, reps))
    sign = jnp.where((jnp.arange(LANES) % HEAD_DIM) < half, -1.0, 1.0).astype(F32)
    sin = jnp.tile(jnp.sin(ang), (1, reps)) * sign[None, :]
    return cos, sin


def _qk_branch(h, w_in, layer, gq, gk, cos, sin):
    s, d = h.shape
    tm, tn = Q_TILE, COL_TILE
    grp = jnp.arange(tn) // HEAD_DIM
    gsum = (grp[:, None] == grp[None, :]).astype(BF16)
    gq_t = jnp.tile(gq, tn // HEAD_DIM)[None, :]
    gk_t = jnp.tile(gk, tn // HEAD_DIM)[None, :]
    const = lambda shape: pl.BlockSpec(shape, lambda n, m: (0, 0))
    return pl.pallas_call(
        _qk_kernel,
        grid=(D_ATTN // tn, s // tm),
        in_specs=[
            pl.BlockSpec((tm, d), lambda n, m: (m, 0)),
            _wspec(layer, d, tn, OFF_Q), _wspec(layer, d, tn, OFF_K),
            const((tn, tn)), const((1, tn)), const((1, tn)),
            pl.BlockSpec((tm, LANES), lambda n, m: (m, 0)),
            pl.BlockSpec((tm, LANES), lambda n, m: (m, 0)),
        ],
        out_specs=[
            pl.BlockSpec((None, tn, tm), lambda n, m: (m, n, 0)),
            pl.BlockSpec((tm, tn), lambda n, m: (m, n)),
        ],
        out_shape=[
            jax.ShapeDtypeStruct((s // tm, D_ATTN, tm), BF16),
            jax.ShapeDtypeStruct((s, D_ATTN), BF16),
        ],
        scratch_shapes=[pltpu.VMEM((d, tn), BF16)] * 2,
        compiler_params=_params(("arbitrary", "arbitrary"), 56),
        name="qk_branch",
    )(h, w_in, w_in, gsum, gq_t, gk_t, cos, sin)


def _vt_kernel(h_ref, w_ref, o_ref, w_b):
    _cast_weights_once([(w_ref, w_b)])
    vt = _dot(h_ref[...], w_b[...]).astype(BF16).T
    for hh in range(o_ref.shape[0]):
        for t in range(o_ref.shape[1]):
            o_ref[hh, t] = vt[hh * V_DIM:(hh + 1) * V_DIM, t * KV_TILE:(t + 1) * KV_TILE]


def _v_branch(h, w_in, layer):
    s, d = h.shape
    tm, tn = WIDE_ROW_TILE, COL_TILE
    return pl.pallas_call(
        _vt_kernel,
        grid=(D_ATTN // tn, s // tm),
        in_specs=[pl.BlockSpec((tm, d), lambda n, m: (m, 0)), _wspec(layer, d, tn, OFF_V)],
        out_specs=pl.BlockSpec((tn // V_DIM, tm // KV_TILE, V_DIM, KV_TILE),
                               lambda n, m: (n, m, 0, 0)),
        out_shape=jax.ShapeDtypeStruct((N_HEADS, s // KV_TILE, V_DIM, KV_TILE), BF16),
        scratch_shapes=[pltpu.VMEM((d, tn), BF16)],
        compiler_params=_params(("arbitrary", "arbitrary"), 56),
        name="v_branch",
    )(h, w_in)


def _act_kernel(h_ref, w_ref, o_ref, w_b, *, act):
    _cast_weights_once([(w_ref, w_b)])
    r = _dot(h_ref[...], w_b[...])
    sg = _sigmoid(r)
    o_ref[...] = (r * sg if act == "silu" else sg).astype(BF16)


def _act_branch(h, w_in, layer, off, width, act):
    s, d = h.shape
    tm, tn = WIDE_ROW_TILE, 2 * COL_TILE
    return pl.pallas_call(
        functools.partial(_act_kernel, act=act),
        grid=(width // tn, s // tm),
        in_specs=[pl.BlockSpec((tm, d), lambda n, m: (m, 0)), _wspec(layer, d, tn, off)],
        out_specs=pl.BlockSpec((tm, tn), lambda n, m: (m, n)),
        out_shape=jax.ShapeDtypeStruct((s, width), BF16),
        scratch_shapes=[pltpu.VMEM((d, tn), BF16)],
        compiler_params=_params(("arbitrary", "arbitrary"), 56),
        name="proj_" + act,
    )(h, w_in)


def _attn_kernel(lamv_ref, qt_ref, k_ref, vt_ref, zs_ref, g_ref, o_ref,
                 s_ref, smax_ref, sd_ref, sdmax_ref, m_ref, acc_ref, *, lambda_init):
    n_tiles, _, tq = qt_ref.shape
    nh, _, _, tk = vt_ref.shape
    heads = range(nh)

    comp1 = lax.broadcasted_iota(jnp.int32, (V_DIM, tq), 0) < HEAD_DIM
    ones = jnp.ones((BF16_SUBLANES, tk), BF16)
    mask = (lax.broadcasted_iota(jnp.int32, (tk, tq), 0) // CHUNK
            <= lax.broadcasted_iota(jnp.int32, (tk, tq), 1) // CHUNK)
    lamv = lamv_ref[...]
    lam = (jnp.exp(jnp.sum(lamv[0:1] * lamv[1:2], axis=-1, keepdims=True))
           - jnp.exp(jnp.sum(lamv[2:3] * lamv[3:4], axis=-1, keepdims=True))
           + lambda_init)

    def masked_queries(i, hh, c):
        qt = qt_ref[i, hh * V_DIM:(hh + 1) * V_DIM, :]
        zero = jnp.zeros_like(qt)
        return jnp.where(comp1, qt, zero) if c == 0 else jnp.where(comp1, zero, qt)

    def diagonal_scores(i, hh):
        kd = k_ref[pl.ds(pl.multiple_of(i * tk, tk), tk), hh * V_DIM:(hh + 1) * V_DIM]
        sd = jnp.where(mask, _dot(kd, masked_queries(i, hh, 1)), -jnp.inf)
        sd_ref[hh] = sd
        sdmax_ref[hh] = jnp.max(sd, axis=0, keepdims=True)

    def scores(qz, hh, c, j):
        kj = k_ref[pl.ds(pl.multiple_of(j * tk, tk), tk), hh * V_DIM:(hh + 1) * V_DIM]
        s = _dot(kj, qz)
        s_ref[2 * hh + c] = s
        smax_ref[2 * hh + c] = jnp.max(s, axis=0, keepdims=True)

    def consume(hh, c, j, remask=False, diag=False):
        ci = 2 * hh + c
        s = sd_ref[hh] if diag else s_ref[ci]
        smax = sdmax_ref[hh] if diag else smax_ref[ci]
        if remask:
            s = jnp.where(mask, s, -jnp.inf)
            smax = jnp.max(s, axis=0, keepdims=True)
        m_old = m_ref[ci]
        m_new = jnp.maximum(m_old, smax)
        m_ref[ci] = m_new
        p = jnp.exp2(s - m_new).astype(BF16)
        vj = jnp.concatenate([vt_ref[hh, j], ones], axis=0)
        acc_ref[ci] = acc_ref[ci] * jnp.exp2(m_old - m_new) + _dot(vj, p)

    def query_tile(i, carry):
        qz = {(hh, c): masked_queries(i, hh, c) for hh in heads for c in range(2)}

        def full_tile(t):
            for hh in heads:
                scores(qz[hh, 1], hh, 1, t)
            for hh in heads:
                consume(hh, 0, t)
            for hh in heads:
                scores(qz[hh, 0], hh, 0, t + 1)
            for hh in heads:
                consume(hh, 1, t)

        m_ref[...] = jnp.full(m_ref.shape, -jnp.inf, F32)
        acc_ref[...] = jnp.zeros(acc_ref.shape, F32)

        def four_tiles(u, c):
            for r in range(4):
                full_tile(4 * u + r)
            return c

        lax.fori_loop(0, jnp.right_shift(i, 2), four_tiles, 0)

        @pl.when(jnp.bitwise_and(i, 2) == 2)
        def _():
            full_tile(jnp.bitwise_and(i, -4))
            full_tile(jnp.bitwise_and(i, -4) + 1)

        @pl.when(jnp.bitwise_and(i, 1) == 1)
        def _():
            full_tile(i - 1)

        for hh in heads:
            consume(hh, 0, i, remask=True)
        for hh in heads:
            consume(hh, 1, i, diag=True)
        nxt = jnp.minimum(i + 1, n_tiles - 1)
        for hh in heads:
            scores(masked_queries(nxt, hh, 0), hh, 0, 0)
        for hh in heads:
            diagonal_scores(nxt, hh)

        rows = pl.ds(pl.multiple_of(i * tq, tq), tq)
        for hh in heads:
            o1 = acc_ref[2 * hh, :V_DIM, :] / acc_ref[2 * hh, V_DIM:V_DIM + 1, :]
            o2 = acc_ref[2 * hh + 1, :V_DIM, :] / acc_ref[2 * hh + 1, V_DIM:V_DIM + 1, :]
            d = (o1 - lam * o2).T
            dn = d * lax.rsqrt(jnp.mean(d * d, axis=-1, keepdims=True) + EPS) * g_ref[...]
            cols = slice(hh * V_DIM, (hh + 1) * V_DIM)
            o_ref[rows, cols] = (dn * (1.0 - lambda_init)
                                 * zs_ref[rows, cols].astype(F32)).astype(BF16)
        return carry

    for hh in heads:
        scores(masked_queries(0, hh, 0), hh, 0, 0)
        diagonal_scores(0, hh)
    lax.fori_loop(0, n_tiles, query_tile, 0)


def _attention(lamv, qt, k, vt, zs, subln_g, lambda_init):
    s = k.shape[0]
    assert Q_TILE == KV_TILE and Q_TILE % CHUNK == 0
    nh = ATTN_HEADS_PER_STEP
    chains = 2 * nh
    return pl.pallas_call(
        functools.partial(_attn_kernel, lambda_init=lambda_init),
        grid=(N_HEADS // nh,),
        in_specs=[
            pl.BlockSpec((4, HEAD_DIM), lambda g: (0, 0)),
            pl.BlockSpec((s // Q_TILE, nh * V_DIM, Q_TILE), lambda g: (0, g, 0)),
            pl.BlockSpec((s, nh * V_DIM), lambda g: (0, g)),
            pl.BlockSpec((nh, s // KV_TILE, V_DIM, KV_TILE), lambda g: (g, 0, 0, 0)),
            pl.BlockSpec((s, nh * V_DIM), lambda g: (0, g)),
            pl.BlockSpec((1, V_DIM), lambda g: (0, 0)),
        ],
        out_specs=pl.BlockSpec((s, nh * V_DIM), lambda g: (0, g)),
        out_shape=jax.ShapeDtypeStruct((s, D_ATTN), BF16),
        scratch_shapes=[
            pltpu.VMEM((chains, KV_TILE, Q_TILE), F32),
            pltpu.VMEM((chains, 1, Q_TILE), F32),
            pltpu.VMEM((nh, KV_TILE, Q_TILE), F32),
            pltpu.VMEM((nh, 1, Q_TILE), F32),
            pltpu.VMEM((chains, 1, Q_TILE), F32),
            pltpu.VMEM((chains, V_DIM + BF16_SUBLANES, Q_TILE), F32),
        ],
        compiler_params=_params(("parallel",), 60),
        name="diff_attn",
    )(lamv, qt, k, vt, zs, subln_g)


def _out_kernel(yc_ref, oz_ref, ga_ref, gb_ref, wc_ref, wa_ref, wo_ref, x_ref, gate_ref,
                *rest, emit_next_h):
    a = _dot(yc_ref[...], wc_ref[...])
    b = _dot(oz_ref[...], wa_ref[...])
    merged = (ga_ref[...].astype(F32) * a + gb_ref[...].astype(F32) * b).astype(BF16)
    x_new = x_ref[...] + gate_ref[...] * _dot(merged, wo_ref[...])
    if emit_next_h:
        g_ref, scale_ref, shift_ref, o_ref, h_ref = rest
        _norm_kernel(x_new, g_ref, scale_ref, shift_ref, h_ref)
    else:
        o_ref, = rest
    o_ref[...] = x_new


def _merge_out(yc, oz, gates, wc, wa, wo, layer, x, gate, next_norm=None):
    s, d = x.shape
    tm = ROW_TILE
    resident = lambda rows: pl.BlockSpec((None, rows, d), lambda m: (layer, 0, 0),
                                         pipeline_mode=pl.Buffered(1))
    row = pl.BlockSpec((1, d), lambda m: (0, 0))
    tile = pl.BlockSpec((tm, d), lambda m: (m, 0))
    emit = next_norm is not None
    return pl.pallas_call(
        functools.partial(_out_kernel, emit_next_h=emit),
        grid=(s // tm,),
        in_specs=[
            pl.BlockSpec((tm, D_CONV), lambda m: (m, 0)),
            pl.BlockSpec((tm, D_ATTN), lambda m: (m, 0)),
            pl.BlockSpec((tm, d), lambda m: (m, 0)),
            pl.BlockSpec((tm, d), lambda m: (m, 1)),
            resident(D_CONV), resident(D_ATTN), resident(d),
            tile, row,
        ] + ([row, row, row] if emit else []),
        out_specs=[tile, tile] if emit else tile,
        out_shape=([jax.ShapeDtypeStruct((s, d), F32), jax.ShapeDtypeStruct((s, d), BF16)]
                   if emit else jax.ShapeDtypeStruct((s, d), F32)),
        compiler_params=_params(("parallel",), 60),
        name="merge_out",
    )(yc, oz, gates, gates, wc, wa, wo, x, gate, *(next_norm if emit else ()))


def kernel(x, c, ada_w, ada_b, norm_g, w_in, conv_w, w_conv_out, q_norm_g, k_norm_g,
           lam_q1, lam_k1, lam_q2, lam_k2, subln_g, w_attn_out, w_o):
    bsz, seq, d = x.shape
    depth = w_in.shape[0]
    assert bsz == 1 and d == D_MODEL and seq % WIDE_ROW_TILE == 0 and seq % Q_TILE == 0
    xs = x[0]
    mod = _modulation(c, ada_w, ada_b)
    cos, sin = _rope_tables(seq)
    wc_b = w_conv_out.astype(BF16)
    wa_b = w_attn_out.astype(BF16)
    wo_b = w_o.astype(BF16)
    lamv = jnp.stack([lam_q1, lam_k1, lam_q2, lam_k2], axis=1).astype(F32)
    mods = [tuple(mod[l, :, i * d:(i + 1) * d] for i in range(3)) for l in range(depth)]
    h = _norm_mod(xs, norm_g[0][None, :], mods[0][1], mods[0][0])
    for l in range(depth):
        gate = mods[l][2]
        yc = _conv_branch(h, w_in, l, conv_w[l])
        qt, k = _qk_branch(h, w_in, l, q_norm_g[l], k_norm_g[l], cos, sin)
        vt = _v_branch(h, w_in, l)
        zs = _act_branch(h, w_in, l, OFF_ZB, D_ATTN, "silu")
        gates = _act_branch(h, w_in, l, OFF_G, 2 * d, "sigmoid")
        lambda_init = 0.8 - 0.6 * math.exp(-0.3 * l)
        oz = _attention(lamv[l], qt, k, vt, zs, subln_g[l][None, :], lambda_init)
        if l + 1 < depth:
            nxt = (norm_g[l + 1][None, :], mods[l + 1][1], mods[l + 1][0])
            xs, h = _merge_out(yc, oz, gates, wc_b, wa_b, wo_b, l, xs, gate, nxt)
        else:
            xs = _merge_out(yc, oz, gates, wc_b, wa_b, wo_b, l, xs, gate)
    return xs[None]
```

```python
import functools
import math

import jax
import jax.numpy as jnp
from jax import lax
from jax.experimental import pallas as pl
from jax.experimental.pallas import tpu as pltpu

F32 = jnp.float32
BF16 = jnp.bfloat16

D_MODEL = 2048
D_CONV = 1024
N_HEADS = 8
HEAD_DIM = 64
V_DIM = 2 * HEAD_DIM
D_ATTN = N_HEADS * V_DIM
CHUNK = 64
ROPE_THETA = 10000.0
EPS = 1e-6
LOG2_E = math.log2(math.e)
OFF_U, OFF_BG, OFF_CG, OFF_ZA = 0, D_CONV, 2 * D_CONV, 3 * D_CONV
OFF_Q = 4 * D_CONV
OFF_K = OFF_Q + D_ATTN
OFF_V = OFF_K + D_ATTN
OFF_ZB = OFF_V + D_ATTN
OFF_G = OFF_ZB + D_ATTN

LANES = 128
F32_SUBLANES = 8
BF16_SUBLANES = 16
MIB = 1024 * 1024

ROW_TILE = 512
WIDE_ROW_TILE = 1024
COL_TILE = 512
CONV_CHUNK = 256
Q_TILE = 512
KV_TILE = 512
ATTN_HEADS_PER_STEP = 2
MOD_COL_TILE = 768


def _params(semantics, vmem_mib):
    return pltpu.CompilerParams(dimension_semantics=semantics,
                                vmem_limit_bytes=vmem_mib * MIB)


def _dot(a, b):
    return jnp.dot(a, b, preferred_element_type=F32)


def _sigmoid(x):
    return 0.5 * jnp.tanh(0.5 * x) + 0.5


def _wspec(layer, rows, tn, off):
    return pl.BlockSpec((None, rows, tn), lambda n, m: (layer, 0, off // tn + n))


def _cast_weights_once(pairs):
    @pl.when(pl.program_id(1) == 0)
    def _():
        for w_ref, wb_ref in pairs:
            wb_ref[...] = w_ref[...].astype(BF16)


def _mod_kernel(c_ref, w_ref, b_ref, o_ref):
    c = c_ref[...]
    o_ref[0] = _dot(c * jax.nn.sigmoid(c), w_ref[0]) + b_ref[0]


def _modulation(c, ada_w, ada_b):
    depth, d, n = ada_w.shape
    c8 = jnp.broadcast_to(c, (8, d))
    out = pl.pallas_call(
        _mod_kernel,
        grid=(depth, n // MOD_COL_TILE),
        in_specs=[
            pl.BlockSpec((8, d), lambda l, j: (0, 0)),
            pl.BlockSpec((1, d, MOD_COL_TILE), lambda l, j: (l, 0, j)),
            pl.BlockSpec((1, 1, MOD_COL_TILE), lambda l, j: (l, 0, j)),
        ],
        out_specs=pl.BlockSpec((1, 8, MOD_COL_TILE), lambda l, j: (l, 0, j)),
        out_shape=jax.ShapeDtypeStruct((depth, 8, n), F32),
        compiler_params=_params(("parallel", "parallel"), 32),
        name="adaln_mod",
    )(c8, ada_w, ada_b.reshape(depth, 1, n))
    return out[:, 0:1, :]


def _norm_kernel(x_ref, g_ref, scale_ref, shift_ref, h_ref):
    x = x_ref[...]
    y = x * lax.rsqrt(jnp.mean(x * x, axis=-1, keepdims=True) + EPS) * g_ref[...]
    h_ref[...] = (y * (1.0 + scale_ref[...]) + shift_ref[...]).astype(BF16)


def _norm_mod(x, g, scale, shift):
    s, d = x.shape
    row = pl.BlockSpec((1, d), lambda m: (0, 0))
    return pl.pallas_call(
        _norm_kernel,
        grid=(s // ROW_TILE,),
        in_specs=[pl.BlockSpec((ROW_TILE, d), lambda m: (m, 0)), row, row, row],
        out_specs=pl.BlockSpec((ROW_TILE, d), lambda m: (m, 0)),
        out_shape=jax.ShapeDtypeStruct((s, d), BF16),
        compiler_params=_params(("parallel",), 32),
        name="norm_mod",
    )(x, g, scale, shift)


def _conv_kernel(h_ref, wu_ref, wb_ref, wc_ref, wz_ref, cw_ref, o_ref,
                 wu_b, wb_b, wc_b, wz_b, tail_ref):
    _cast_weights_once([(wu_ref, wu_b), (wb_ref, wb_b), (wc_ref, wc_b), (wz_ref, wz_b)])

    @pl.when(pl.program_id(1) == 0)
    def _():
        tail_ref[...] = jnp.zeros(tail_ref.shape, F32)

    h = h_ref[...]
    tm, tn = o_ref.shape
    row = lax.broadcasted_iota(jnp.int32, (tm, CONV_CHUNK), 0)
    for c in range(0, tn, CONV_CHUNK):
        cols = slice(c, c + CONV_CHUNK)
        v = _dot(h, wc_b[:, cols]) * _dot(h, wu_b[:, cols])
        tail = tail_ref[:, cols]
        tail_ref[:, cols] = v[tm - F32_SUBLANES:, :]
        prev1 = tail[F32_SUBLANES - 1:F32_SUBLANES, :]
        prev2 = tail[F32_SUBLANES - 2:F32_SUBLANES - 1, :]
        v1 = jnp.where(row == 0, prev1, pltpu.roll(v, 1, axis=0))
        v2 = jnp.where(row == 0, prev2,
                       jnp.where(row == 1, prev1, pltpu.roll(v, 2, axis=0)))
        cw = cw_ref[:, cols]
        y = cw[0:1, :] * v2 + cw[1:2, :] * v1 + cw[2:3, :] * v
        za = _dot(h, wz_b[:, cols])
        o_ref[:, cols] = (_dot(h, wb_b[:, cols]) * y * (za * _sigmoid(za))).astype(BF16)


def _conv_branch(h, w_in, layer, conv_w):
    s, d = h.shape
    tm, tn = ROW_TILE, COL_TILE
    return pl.pallas_call(
        _conv_kernel,
        grid=(D_CONV // tn, s // tm),
        in_specs=[
            pl.BlockSpec((tm, d), lambda n, m: (m, 0)),
            _wspec(layer, d, tn, OFF_U), _wspec(layer, d, tn, OFF_BG),
            _wspec(layer, d, tn, OFF_CG), _wspec(layer, d, tn, OFF_ZA),
            pl.BlockSpec((3, tn), lambda n, m: (0, n)),
        ],
        out_specs=pl.BlockSpec((tm, tn), lambda n, m: (m, n)),
        out_shape=jax.ShapeDtypeStruct((s, D_CONV), BF16),
        scratch_shapes=[pltpu.VMEM((d, tn), BF16)] * 4
                       + [pltpu.VMEM((F32_SUBLANES, tn), F32)],
        compiler_params=_params(("arbitrary", "arbitrary"), 60),
        name="conv_branch",
    )(h, w_in, w_in, w_in, w_in, conv_w)


def _group_norm_rope(r, gsum, g, cos, sin):
    ss = _dot((r * r).astype(BF16), gsum)
    rn = r * lax.rsqrt(ss * (1.0 / HEAD_DIM) + EPS) * g
    half = HEAD_DIM // 2
    lane = lax.broadcasted_iota(jnp.int32, (r.shape[0], LANES), 1)
    first_half = (lane % HEAD_DIM) < half
    parts = []
    for a in range(r.shape[1] // LANES):
        x = rn[:, a * LANES:(a + 1) * LANES]
        partner = jnp.where(first_half, pltpu.roll(x, LANES - half, axis=1),
                            pltpu.roll(x, half, axis=1))
        parts.append(x * cos + partner * sin)
    return jnp.concatenate(parts, axis=1)


def _qk_kernel(h_ref, wq_ref, wk_ref, gsum_ref, gq_ref, gk_ref, cos_ref, sin_ref,
               qt_ref, k_ref, wq_b, wk_b):
    _cast_weights_once([(wq_ref, wq_b), (wk_ref, wk_b)])
    h = h_ref[...]
    gsum = gsum_ref[...]
    cos = cos_ref[...]
    sin = sin_ref[...]
    q = _group_norm_rope(_dot(h, wq_b[...]), gsum, gq_ref[...], cos, sin)
    qt_ref[...] = (q * (HEAD_DIM ** -0.5 * LOG2_E)).T.astype(BF16)
    k = _group_norm_rope(_dot(h, wk_b[...]), gsum, gk_ref[...], cos, sin)
    k_ref[...] = k.astype(BF16)


def _rope_tables(s):
    half = HEAD_DIM // 2
    inv = ROPE_THETA ** (-jnp.arange(half, dtype=F32) / half)
    ang = jnp.arange(s).astype(F32)[:, None] * inv[None, :]
    reps = LANES // half
    cos = jnp.tile(jnp.cos(ang), (1, reps))
    sign = jnp.where((jnp.arange(LANES) % HEAD_DIM) < half, -1.0, 1.0).astype(F32)
    sin = jnp.tile(jnp.sin(ang), (1, reps)) * sign[None, :]
    return cos, sin


def _qk_branch(h, w_in, layer, gq, gk, cos, sin):
    s, d = h.shape
    tm, tn = Q_TILE, COL_TILE
    grp = jnp.arange(tn) // HEAD_DIM
    gsum = (grp[:, None] == grp[None, :]).astype(BF16)
    gq_t = jnp.tile(gq, tn // HEAD_DIM)[None, :]
    gk_t = jnp.tile(gk, tn // HEAD_DIM)[None, :]
    const = lambda shape: pl.BlockSpec(shape, lambda n, m: (0, 0))
    return pl.pallas_call(
        _qk_kernel,
        grid=(D_ATTN // tn, s // tm),
        in_specs=[
            pl.BlockSpec((tm, d), lambda n, m: (m, 0)),
            _wspec(layer, d, tn, OFF_Q), _wspec(layer, d, tn, OFF_K),
            const((tn, tn)), const((1, tn)), const((1, tn)),
            pl.BlockSpec((tm, LANES), lambda n, m: (m, 0)),
            pl.BlockSpec((tm, LANES), lambda n, m: (m, 0)),
        ],
        out_specs=[
            pl.BlockSpec((None, tn, tm), lambda n, m: (m, n, 0)),
            pl.BlockSpec((tm, tn), lambda n, m: (m, n)),
        ],
        out_shape=[
            jax.ShapeDtypeStruct((s // tm, D_ATTN, tm), BF16),
            jax.ShapeDtypeStruct((s, D_ATTN), BF16),
        ],
        scratch_shapes=[pltpu.VMEM((d, tn), BF16)] * 2,
        compiler_params=_params(("arbitrary", "arbitrary"), 56),
        name="qk_branch",
    )(h, w_in, w_in, gsum, gq_t, gk_t, cos, sin)


def _vt_kernel(h_ref, w_ref, o_ref, w_b):
    _cast_weights_once([(w_ref, w_b)])
    vt = _dot(h_ref[...], w_b[...]).astype(BF16).T
    for hh in range(o_ref.shape[0]):
        for t in range(o_ref.shape[1]):
            o_ref[hh, t] = vt[hh * V_DIM:(hh + 1) * V_DIM, t * KV_TILE:(t + 1) * KV_TILE]


def _v_branch(h, w_in, layer):
    s, d = h.shape
    tm, tn = WIDE_ROW_TILE, COL_TILE
    return pl.pallas_call(
        _vt_kernel,
        grid=(D_ATTN // tn, s // tm),
        in_specs=[pl.BlockSpec((tm, d), lambda n, m: (m, 0)), _wspec(layer, d, tn, OFF_V)],
        out_specs=pl.BlockSpec((tn // V_DIM, tm // KV_TILE, V_DIM, KV_TILE),
                               lambda n, m: (n, m, 0, 0)),
        out_shape=jax.ShapeDtypeStruct((N_HEADS, s // KV_TILE, V_DIM, KV_TILE), BF16),
        scratch_shapes=[pltpu.VMEM((d, tn), BF16)],
        compiler_params=_params(("arbitrary", "arbitrary"), 56),
        name="v_branch",
    )(h, w_in)


def _act_kernel(h_ref, w_ref, o_ref, w_b, *, act):
    _cast_weights_once([(w_ref, w_b)])
    r = _dot(h_ref[...], w_b[...])
    sg = _sigmoid(r)
    o_ref[...] = (r * sg if act == "silu" else sg).astype(BF16)


def _act_branch(h, w_in, layer, off, width, act):
    s, d = h.shape
    tm, tn = WIDE_ROW_TILE, 2 * COL_TILE
    return pl.pallas_call(
        functools.partial(_act_kernel, act=act),
        grid=(width // tn, s // tm),
        in_specs=[pl.BlockSpec((tm, d), lambda n, m: (m, 0)), _wspec(layer, d, tn, off)],
        out_specs=pl.BlockSpec((tm, tn), lambda n, m: (m, n)),
        out_shape=jax.ShapeDtypeStruct((s, width), BF16),
        scratch_shapes=[pltpu.VMEM((d, tn), BF16)],
        compiler_params=_params(("arbitrary", "arbitrary"), 56),
        name="proj_" + act,
    )(h, w_in)


def _attn_kernel(lamv_ref, qt_ref, k_ref, vt_ref, zs_ref, g_ref, o_ref,
                 s_ref, smax_ref, sd_ref, sdmax_ref, m_ref, acc_ref, *, lambda_init):
    n_tiles, _, tq = qt_ref.shape
    nh, _, _, tk = vt_ref.shape
    heads = range(nh)

    comp1 = lax.broadcasted_iota(jnp.int32, (V_DIM, tq), 0) < HEAD_DIM
    ones = jnp.ones((BF16_SUBLANES, tk), BF16)
    mask = (lax.broadcasted_iota(jnp.int32, (tk, tq), 0) // CHUNK
            <= lax.broadcasted_iota(jnp.int32, (tk, tq), 1) // CHUNK)
    lamv = lamv_ref[...]
    lam = (jnp.exp(jnp.sum(lamv[0:1] * lamv[1:2], axis=-1, keepdims=True))
           - jnp.exp(jnp.sum(lamv[2:3] * lamv[3:4], axis=-1, keepdims=True))
           + lambda_init)

    def masked_queries(i, hh, c):
        qt = qt_ref[i, hh * V_DIM:(hh + 1) * V_DIM, :]
        zero = jnp.zeros_like(qt)
        return jnp.where(comp1, qt, zero) if c == 0 else jnp.where(comp1, zero, qt)

    def diagonal_scores(i, hh):
        kd = k_ref[pl.ds(pl.multiple_of(i * tk, tk), tk), hh * V_DIM:(hh + 1) * V_DIM]
        sd = jnp.where(mask, _dot(kd, masked_queries(i, hh, 1)), -jnp.inf)
        sd_ref[hh] = sd
        sdmax_ref[hh] = jnp.max(sd, axis=0, keepdims=True)

    def scores(qz, hh, c, j):
        kj = k_ref[pl.ds(pl.multiple_of(j * tk, tk), tk), hh * V_DIM:(hh + 1) * V_DIM]
        s = _dot(kj, qz)
        s_ref[2 * hh + c] = s
        smax_ref[2 * hh + c] = jnp.max(s, axis=0, keepdims=True)

    def consume(hh, c, j, remask=False, diag=False):
        ci = 2 * hh + c
        s = sd_ref[hh] if diag else s_ref[ci]
        smax = sdmax_ref[hh] if diag else smax_ref[ci]
        if remask:
            s = jnp.where(mask, s, -jnp.inf)
            smax = jnp.max(s, axis=0, keepdims=True)
        m_old = m_ref[ci]
        m_new = jnp.maximum(m_old, smax)
        m_ref[ci] = m_new
        p = jnp.exp2(s - m_new).astype(BF16)
        vj = jnp.concatenate([vt_ref[hh, j], ones], axis=0)
        acc_ref[ci] = acc_ref[ci] * jnp.exp2(m_old - m_new) + _dot(vj, p)

    def query_tile(i, carry):
        qz = {(hh, c): masked_queries(i, hh, c) for hh in heads for c in range(2)}

        def full_tile(t):
            for hh in heads:
                scores(qz[hh, 1], hh, 1, t)
            for hh in heads:
                consume(hh, 0, t)
            for hh in heads:
                scores(qz[hh, 0], hh, 0, t + 1)
            for hh in heads:
                consume(hh, 1, t)

        m_ref[...] = jnp.full(m_ref.shape, -jnp.inf, F32)
        acc_ref[...] = jnp.zeros(acc_ref.shape, F32)

        def eight_tiles(u, c):
            for r in range(8):
                full_tile(8 * u + r)
            return c

        lax.fori_loop(0, jnp.right_shift(i, 3), eight_tiles, 0)

        @pl.when(jnp.bitwise_and(i, 4) == 4)
        def _():
            for r in range(4):
                full_tile(jnp.bitwise_and(i, -8) + r)

        @pl.when(jnp.bitwise_and(i, 2) == 2)
        def _():
            full_tile(jnp.bitwise_and(i, -4))
            full_tile(jnp.bitwise_and(i, -4) + 1)

        @pl.when(jnp.bitwise_and(i, 1) == 1)
        def _():
            full_tile(i - 1)

        for hh in heads:
            consume(hh, 0, i, remask=True)
        for hh in heads:
            consume(hh, 1, i, diag=True)
        nxt = jnp.minimum(i + 1, n_tiles - 1)
        for hh in heads:
            scores(masked_queries(nxt, hh, 0), hh, 0, 0)
        for hh in heads:
            diagonal_scores(nxt, hh)

        rows = pl.ds(pl.multiple_of(i * tq, tq), tq)
        for hh in heads:
            o1 = acc_ref[2 * hh, :V_DIM, :] / acc_ref[2 * hh, V_DIM:V_DIM + 1, :]
            o2 = acc_ref[2 * hh + 1, :V_DIM, :] / acc_ref[2 * hh + 1, V_DIM:V_DIM + 1, :]
            d = (o1 - lam * o2).T
            dn = d * lax.rsqrt(jnp.mean(d * d, axis=-1, keepdims=True) + EPS) * g_ref[...]
            cols = slice(hh * V_DIM, (hh + 1) * V_DIM)
            o_ref[rows, cols] = (dn * (1.0 - lambda_init)
                                 * zs_ref[rows, cols].astype(F32)).astype(BF16)
        return carry

    for hh in heads:
        scores(masked_queries(0, hh, 0), hh, 0, 0)
        diagonal_scores(0, hh)
    lax.fori_loop(0, n_tiles, query_tile, 0)


def _attention(lamv, qt, k, vt, zs, subln_g, lambda_init):
    s = k.shape[0]
    assert Q_TILE == KV_TILE and Q_TILE % CHUNK == 0
    nh = ATTN_HEADS_PER_STEP
    chains = 2 * nh
    return pl.pallas_call(
        functools.partial(_attn_kernel, lambda_init=lambda_init),
        grid=(N_HEADS // nh,),
        in_specs=[
            pl.BlockSpec((4, HEAD_DIM), lambda g: (0, 0)),
            pl.BlockSpec((s // Q_TILE, nh * V_DIM, Q_TILE), lambda g: (0, g, 0)),
            pl.BlockSpec((s, nh * V_DIM), lambda g: (0, g)),
            pl.BlockSpec((nh, s // KV_TILE, V_DIM, KV_TILE), lambda g: (g, 0, 0, 0)),
            pl.BlockSpec((s, nh * V_DIM), lambda g: (0, g)),
            pl.BlockSpec((1, V_DIM), lambda g: (0, 0)),
        ],
        out_specs=pl.BlockSpec((s, nh * V_DIM), lambda g: (0, g)),
        out_shape=jax.ShapeDtypeStruct((s, D_ATTN), BF16),
        scratch_shapes=[
            pltpu.VMEM((chains, KV_TILE, Q_TILE), F32),
            pltpu.VMEM((chains, 1, Q_TILE), F32),
            pltpu.VMEM((nh, KV_TILE, Q_TILE), F32),
            pltpu.VMEM((nh, 1, Q_TILE), F32),
            pltpu.VMEM((chains, 1, Q_TILE), F32),
            pltpu.VMEM((chains, V_DIM + BF16_SUBLANES, Q_TILE), F32),
        ],
        compiler_params=_params(("parallel",), 60),
        name="diff_attn",
    )(lamv, qt, k, vt, zs, subln_g)


def _out_kernel(yc_ref, oz_ref, ga_ref, gb_ref, wc_ref, wa_ref, wo_ref, x_ref, gate_ref,
                *rest, emit_next_h):
    a = _dot(yc_ref[...], wc_ref[...])
    b = _dot(oz_ref[...], wa_ref[...])
    merged = (ga_ref[...].astype(F32) * a + gb_ref[...].astype(F32) * b).astype(BF16)
    x_new = x_ref[...] + gate_ref[...] * _dot(merged, wo_ref[...])
    if emit_next_h:
        g_ref, scale_ref, shift_ref, o_ref, h_ref = rest
        _norm_kernel(x_new, g_ref, scale_ref, shift_ref, h_ref)
    else:
        o_ref, = rest
    o_ref[...] = x_new


def _merge_out(yc, oz, gates, wc, wa, wo, layer, x, gate, next_norm=None):
    s, d = x.shape
    tm = ROW_TILE
    resident = lambda rows: pl.BlockSpec((None, rows, d), lambda m: (layer, 0, 0),
                                         pipeline_mode=pl.Buffered(1))
    row = pl.BlockSpec((1, d), lambda m: (0, 0))
    tile = pl.BlockSpec((tm, d), lambda m: (m, 0))
    emit = next_norm is not None
    return pl.pallas_call(
        functools.partial(_out_kernel, emit_next_h=emit),
        grid=(s // tm,),
        in_specs=[
            pl.BlockSpec((tm, D_CONV), lambda m: (m, 0)),
            pl.BlockSpec((tm, D_ATTN), lambda m: (m, 0)),
            pl.BlockSpec((tm, d), lambda m: (m, 0)),
            pl.BlockSpec((tm, d), lambda m: (m, 1)),
            resident(D_CONV), resident(D_ATTN), resident(d),
            tile, row,
        ] + ([row, row, row] if emit else []),
        out_specs=[tile, tile] if emit else tile,
        out_shape=([jax.ShapeDtypeStruct((s, d), F32), jax.ShapeDtypeStruct((s, d), BF16)]
                   if emit else jax.ShapeDtypeStruct((s, d), F32)),
        compiler_params=_params(("parallel",), 60),
        name="merge_out",
    )(yc, oz, gates, gates, wc, wa, wo, x, gate, *(next_norm if emit else ()))


def kernel(x, c, ada_w, ada_b, norm_g, w_in, conv_w, w_conv_out, q_norm_g, k_norm_g,
           lam_q1, lam_k1, lam_q2, lam_k2, subln_g, w_attn_out, w_o):
    bsz, seq, d = x.shape
    depth = w_in.shape[0]
    assert bsz == 1 and d == D_MODEL and seq % WIDE_ROW_TILE == 0 and seq % Q_TILE == 0
    xs = x[0]
    mod = _modulation(c, ada_w, ada_b)
    cos, sin = _rope_tables(seq)
    wc_b = w_conv_out.astype(BF16)
    wa_b = w_attn_out.astype(BF16)
    wo_b = w_o.astype(BF16)
    lamv = jnp.stack([lam_q1, lam_k1, lam_q2, lam_k2], axis=1).astype(F32)
    mods = [tuple(mod[l, :, i * d:(i + 1) * d] for i in range(3)) for l in range(depth)]
    h = _norm_mod(xs, norm_g[0][None, :], mods[0][1], mods[0][0])
    for l in range(depth):
        gate = mods[l][2]
        yc = _conv_branch(h, w_in, l, conv_w[l])
        qt, k = _qk_branch(h, w_in, l, q_norm_g[l], k_norm_g[l], cos, sin)
        vt = _v_branch(h, w_in, l)
        zs = _act_branch(h, w_in, l, OFF_ZB, D_ATTN, "silu")
        gates = _act_branch(h, w_in, l, OFF_G, 2 * d, "sigmoid")
        lambda_init = 0.8 - 0.6 * math.exp(-0.3 * l)
        oz = _attention(lamv[l], qt, k, vt, zs, subln_g[l][None, :], lambda_init)
        if l + 1 < depth:
            nxt = (norm_g[l + 1][None, :], mods[l + 1][1], mods[l + 1][0])
            xs, h = _merge_out(yc, oz, gates, wc_b, wa_b, wo_b, l, xs, gate, nxt)
        else:
            xs = _merge_out(yc, oz, gates, wc_b, wa_b, wo_b, l, xs, gate)
    return xs[None]
```

```python
import functools
import math

import jax
import jax.numpy as jnp
from jax import lax
from jax.experimental import pallas as pl
from jax.experimental.pallas import tpu as pltpu

F32 = jnp.float32
BF16 = jnp.bfloat16

D_MODEL = 2048
D_CONV = 1024
N_HEADS = 8
HEAD_DIM = 64
V_DIM = 2 * HEAD_DIM
D_ATTN = N_HEADS * V_DIM
CHUNK = 64
ROPE_THETA = 10000.0
EPS = 1e-6
LOG2_E = math.log2(math.e)
OFF_U, OFF_BG, OFF_CG, OFF_ZA = 0, D_CONV, 2 * D_CONV, 3 * D_CONV
OFF_Q = 4 * D_CONV
OFF_K = OFF_Q + D_ATTN
OFF_V = OFF_K + D_ATTN
OFF_ZB = OFF_V + D_ATTN
OFF_G = OFF_ZB + D_ATTN

LANES = 128
F32_SUBLANES = 8
BF16_SUBLANES = 16
MIB = 1024 * 1024

ROW_TILE = 512
WIDE_ROW_TILE = 1024
COL_TILE = 512
CONV_CHUNK = 256
GROUP_SUM_WIDTH = 256
Q_TILE = 512
KV_TILE = 512
ATTN_HEADS_PER_STEP = 2
MOD_COL_TILE = 768


def _params(semantics, vmem_mib):
    return pltpu.CompilerParams(dimension_semantics=semantics,
                                vmem_limit_bytes=vmem_mib * MIB)


def _dot(a, b):
    return jnp.dot(a, b, preferred_element_type=F32)


def _sigmoid(x):
    return 0.5 * jnp.tanh(0.5 * x) + 0.5


def _wspec(layer, rows, tn, off):
    return pl.BlockSpec((None, rows, tn), lambda n, m: (layer, 0, off // tn + n))


def _cast_weights_once(pairs):
    @pl.when(pl.program_id(1) == 0)
    def _():
        for w_ref, wb_ref in pairs:
            wb_ref[...] = w_ref[...].astype(BF16)


def _mod_kernel(c_ref, w_ref, b_ref, o_ref):
    c = c_ref[...]
    o_ref[0] = _dot(c * jax.nn.sigmoid(c), w_ref[0]) + b_ref[0]


def _modulation(c, ada_w, ada_b):
    depth, d, n = ada_w.shape
    c8 = jnp.broadcast_to(c, (8, d))
    out = pl.pallas_call(
        _mod_kernel,
        grid=(depth, n // MOD_COL_TILE),
        in_specs=[
            pl.BlockSpec((8, d), lambda l, j: (0, 0)),
            pl.BlockSpec((1, d, MOD_COL_TILE), lambda l, j: (l, 0, j)),
            pl.BlockSpec((1, 1, MOD_COL_TILE), lambda l, j: (l, 0, j)),
        ],
        out_specs=pl.BlockSpec((1, 8, MOD_COL_TILE), lambda l, j: (l, 0, j)),
        out_shape=jax.ShapeDtypeStruct((depth, 8, n), F32),
        compiler_params=_params(("parallel", "parallel"), 32),
        name="adaln_mod",
    )(c8, ada_w, ada_b.reshape(depth, 1, n))
    return out[:, 0:1, :]


def _norm_kernel(x_ref, g_ref, scale_ref, shift_ref, h_ref):
    x = x_ref[...]
    y = x * lax.rsqrt(jnp.mean(x * x, axis=-1, keepdims=True) + EPS) * g_ref[...]
    h_ref[...] = (y * (1.0 + scale_ref[...]) + shift_ref[...]).astype(BF16)


def _norm_mod(x, g, scale, shift):
    s, d = x.shape
    row = pl.BlockSpec((1, d), lambda m: (0, 0))
    return pl.pallas_call(
        _norm_kernel,
        grid=(s // ROW_TILE,),
        in_specs=[pl.BlockSpec((ROW_TILE, d), lambda m: (m, 0)), row, row, row],
        out_specs=pl.BlockSpec((ROW_TILE, d), lambda m: (m, 0)),
        out_shape=jax.ShapeDtypeStruct((s, d), BF16),
        compiler_params=_params(("parallel",), 32),
        name="norm_mod",
    )(x, g, scale, shift)


def _conv_kernel(h_ref, wu_ref, wb_ref, wc_ref, wz_ref, cw_ref, o_ref,
                 wu_b, wb_b, wc_b, wz_b, tail_ref):
    _cast_weights_once([(wu_ref, wu_b), (wb_ref, wb_b), (wc_ref, wc_b), (wz_ref, wz_b)])

    @pl.when(pl.program_id(1) == 0)
    def _():
        tail_ref[...] = jnp.zeros(tail_ref.shape, F32)

    h = h_ref[...]
    tm, tn = o_ref.shape
    row = lax.broadcasted_iota(jnp.int32, (tm, CONV_CHUNK), 0)
    for c in range(0, tn, CONV_CHUNK):
        cols = slice(c, c + CONV_CHUNK)
        v = _dot(h, wc_b[:, cols]) * _dot(h, wu_b[:, cols])
        tail = tail_ref[:, cols]
        tail_ref[:, cols] = v[tm - F32_SUBLANES:, :]
        prev1 = tail[F32_SUBLANES - 1:F32_SUBLANES, :]
        prev2 = tail[F32_SUBLANES - 2:F32_SUBLANES - 1, :]
        v1 = jnp.where(row == 0, prev1, pltpu.roll(v, 1, axis=0))
        v2 = jnp.where(row == 0, prev2,
                       jnp.where(row == 1, prev1, pltpu.roll(v, 2, axis=0)))
        cw = cw_ref[:, cols]
        y = cw[0:1, :] * v2 + cw[1:2, :] * v1 + cw[2:3, :] * v
        za = _dot(h, wz_b[:, cols])
        o_ref[:, cols] = (_dot(h, wb_b[:, cols]) * y * (za * _sigmoid(za))).astype(BF16)


def _conv_branch(h, w_in, layer, conv_w):
    s, d = h.shape
    tm, tn = ROW_TILE, COL_TILE
    return pl.pallas_call(
        _conv_kernel,
        grid=(D_CONV // tn, s // tm),
        in_specs=[
            pl.BlockSpec((tm, d), lambda n, m: (m, 0)),
            _wspec(layer, d, tn, OFF_U), _wspec(layer, d, tn, OFF_BG),
            _wspec(layer, d, tn, OFF_CG), _wspec(layer, d, tn, OFF_ZA),
            pl.BlockSpec((3, tn), lambda n, m: (0, n)),
        ],
        out_specs=pl.BlockSpec((tm, tn), lambda n, m: (m, n)),
        out_shape=jax.ShapeDtypeStruct((s, D_CONV), BF16),
        scratch_shapes=[pltpu.VMEM((d, tn), BF16)] * 4
                       + [pltpu.VMEM((F32_SUBLANES, tn), F32)],
        compiler_params=_params(("arbitrary", "arbitrary"), 60),
        name="conv_branch",
    )(h, w_in, w_in, w_in, w_in, conv_w)


def _group_norm_rope(r, gsum, g, cos, sin):
    rr = (r * r).astype(BF16)
    gw = gsum.shape[0]
    ss = jnp.concatenate([_dot(rr[:, c:c + gw], gsum) for c in range(0, r.shape[1], gw)], axis=1)
    rn = r * lax.rsqrt(ss * (1.0 / HEAD_DIM) + EPS) * g
    half = HEAD_DIM // 2
    lane = lax.broadcasted_iota(jnp.int32, (r.shape[0], LANES), 1)
    first_half = (lane % HEAD_DIM) < half
    parts = []
    for a in range(r.shape[1] // LANES):
        x = rn[:, a * LANES:(a + 1) * LANES]
        partner = jnp.where(first_half, pltpu.roll(x, LANES - half, axis=1),
                            pltpu.roll(x, half, axis=1))
        parts.append(x * cos + partner * sin)
    return jnp.concatenate(parts, axis=1)


def _qk_kernel(h_ref, wq_ref, wk_ref, gsum_ref, gq_ref, gk_ref, cos_ref, sin_ref,
               qt_ref, k_ref, wq_b, wk_b):
    _cast_weights_once([(wq_ref, wq_b), (wk_ref, wk_b)])
    h = h_ref[...]
    gsum = gsum_ref[...]
    cos = cos_ref[...]
    sin = sin_ref[...]
    q = _group_norm_rope(_dot(h, wq_b[...]), gsum, gq_ref[...], cos, sin)
    qt_ref[...] = (q * (HEAD_DIM ** -0.5 * LOG2_E)).T.astype(BF16)
    k = _group_norm_rope(_dot(h, wk_b[...]), gsum, gk_ref[...], cos, sin)
    k_ref[...] = k.astype(BF16)


def _rope_tables(s):
    half = HEAD_DIM // 2
    inv = ROPE_THETA ** (-jnp.arange(half, dtype=F32) / half)
    ang = jnp.arange(s).astype(F32)[:, None] * inv[None, :]
    reps = LANES // half
    cos = jnp.tile(jnp.cos(ang), (1, reps))
    sign = jnp.where((jnp.arange(LANES) % HEAD_DIM) < half, -1.0, 1.0).astype(F32)
    sin = jnp.tile(jnp.sin(ang), (1, reps)) * sign[None, :]
    return cos, sin


def _qk_branch(h, w_in, layer, gq, gk, cos, sin):
    s, d = h.shape
    tm, tn = Q_TILE, COL_TILE
    grp = jnp.arange(GROUP_SUM_WIDTH) // HEAD_DIM
    gsum = (grp[:, None] == grp[None, :]).astype(BF16)
    gq_t = jnp.tile(gq, tn // HEAD_DIM)[None, :]
    gk_t = jnp.tile(gk, tn // HEAD_DIM)[None, :]
    const = lambda shape: pl.BlockSpec(shape, lambda n, m: (0, 0))
    return pl.pallas_call(
        _qk_kernel,
        grid=(D_ATTN // tn, s // tm),
        in_specs=[
            pl.BlockSpec((tm, d), lambda n, m: (m, 0)),
            _wspec(layer, d, tn, OFF_Q), _wspec(layer, d, tn, OFF_K),
            const((GROUP_SUM_WIDTH, GROUP_SUM_WIDTH)), const((1, tn)), const((1, tn)),
            pl.BlockSpec((tm, LANES), lambda n, m: (m, 0)),
            pl.BlockSpec((tm, LANES), lambda n, m: (m, 0)),
        ],
        out_specs=[
            pl.BlockSpec((None, tn, tm), lambda n, m: (m, n, 0)),
            pl.BlockSpec((tm, tn), lambda n, m: (m, n)),
        ],
        out_shape=[
            jax.ShapeDtypeStruct((s // tm, D_ATTN, tm), BF16),
            jax.ShapeDtypeStruct((s, D_ATTN), BF16),
        ],
        scratch_shapes=[pltpu.VMEM((d, tn), BF16)] * 2,
        compiler_params=_params(("arbitrary", "arbitrary"), 56),
        name="qk_branch",
    )(h, w_in, w_in, gsum, gq_t, gk_t, cos, sin)


def _vt_kernel(h_ref, w_ref, o_ref, w_b):
    _cast_weights_once([(w_ref, w_b)])
    vt = _dot(h_ref[...], w_b[...]).astype(BF16).T
    for hh in range(o_ref.shape[0]):
        for t in range(o_ref.shape[1]):
            o_ref[hh, t] = vt[hh * V_DIM:(hh + 1) * V_DIM, t * KV_TILE:(t + 1) * KV_TILE]


def _v_branch(h, w_in, layer):
    s, d = h.shape
    tm, tn = WIDE_ROW_TILE, COL_TILE
    return pl.pallas_call(
        _vt_kernel,
        grid=(D_ATTN // tn, s // tm),
        in_specs=[pl.BlockSpec((tm, d), lambda n, m: (m, 0)), _wspec(layer, d, tn, OFF_V)],
        out_specs=pl.BlockSpec((tn // V_DIM, tm // KV_TILE, V_DIM, KV_TILE),
                               lambda n, m: (n, m, 0, 0)),
        out_shape=jax.ShapeDtypeStruct((N_HEADS, s // KV_TILE, V_DIM, KV_TILE), BF16),
        scratch_shapes=[pltpu.VMEM((d, tn), BF16)],
        compiler_params=_params(("arbitrary", "arbitrary"), 56),
        name="v_branch",
    )(h, w_in)


def _act_kernel(h_ref, w_ref, o_ref, w_b, *, act):
    _cast_weights_once([(w_ref, w_b)])
    r = _dot(h_ref[...], w_b[...])
    sg = _sigmoid(r)
    o_ref[...] = (r * sg if act == "silu" else sg).astype(BF16)


def _act_branch(h, w_in, layer, off, width, act):
    s, d = h.shape
    tm, tn = WIDE_ROW_TILE, 2 * COL_TILE
    return pl.pallas_call(
        functools.partial(_act_kernel, act=act),
        grid=(width // tn, s // tm),
        in_specs=[pl.BlockSpec((tm, d), lambda n, m: (m, 0)), _wspec(layer, d, tn, off)],
        out_specs=pl.BlockSpec((tm, tn), lambda n, m: (m, n)),
        out_shape=jax.ShapeDtypeStruct((s, width), BF16),
        scratch_shapes=[pltpu.VMEM((d, tn), BF16)],
        compiler_params=_params(("arbitrary", "arbitrary"), 56),
        name="proj_" + act,
    )(h, w_in)


def _attn_kernel(lamv_ref, qt_ref, k_ref, vt_ref, zs_ref, g_ref, o_ref,
                 s_ref, smax_ref, sd_ref, sdmax_ref, m_ref, acc_ref, *, lambda_init):
    n_tiles, _, tq = qt_ref.shape
    nh, _, _, tk = vt_ref.shape
    heads = range(nh)

    comp1 = lax.broadcasted_iota(jnp.int32, (V_DIM, tq), 0) < HEAD_DIM
    ones = jnp.ones((BF16_SUBLANES, tk), BF16)
    mask = (lax.broadcasted_iota(jnp.int32, (tk, tq), 0) // CHUNK
            <= lax.broadcasted_iota(jnp.int32, (tk, tq), 1) // CHUNK)
    lamv = lamv_ref[...]
    lam = (jnp.exp(jnp.sum(lamv[0:1] * lamv[1:2], axis=-1, keepdims=True))
           - jnp.exp(jnp.sum(lamv[2:3] * lamv[3:4], axis=-1, keepdims=True))
           + lambda_init)

    def masked_queries(i, hh, c):
        qt = qt_ref[i, hh * V_DIM:(hh + 1) * V_DIM, :]
        zero = jnp.zeros_like(qt)
        return jnp.where(comp1, qt, zero) if c == 0 else jnp.where(comp1, zero, qt)

    def diagonal_scores(i, hh):
        kd = k_ref[pl.ds(pl.multiple_of(i * tk, tk), tk), hh * V_DIM:(hh + 1) * V_DIM]
        sd = jnp.where(mask, _dot(kd, masked_queries(i, hh, 1)), -jnp.inf)
        sd_ref[hh] = sd
        sdmax_ref[hh] = jnp.max(sd, axis=0, keepdims=True)

    def scores(qz, hh, c, j):
        kj = k_ref[pl.ds(pl.multiple_of(j * tk, tk), tk), hh * V_DIM:(hh + 1) * V_DIM]
        s = _dot(kj, qz)
        s_ref[2 * hh + c] = s
        smax_ref[2 * hh + c] = jnp.max(s, axis=0, keepdims=True)

    def consume(hh, c, j, remask=False, diag=False):
        ci = 2 * hh + c
        s = sd_ref[hh] if diag else s_ref[ci]
        smax = sdmax_ref[hh] if diag else smax_ref[ci]
        if remask:
            s = jnp.where(mask, s, -jnp.inf)
            smax = jnp.max(s, axis=0, keepdims=True)
        m_old = m_ref[ci]
        m_new = jnp.maximum(m_old, smax)
        m_ref[ci] = m_new
        p = jnp.exp2(s - m_new).astype(BF16)
        vj = jnp.concatenate([vt_ref[hh, j], ones], axis=0)
        acc_ref[ci] = acc_ref[ci] * jnp.exp2(m_old - m_new) + _dot(vj, p)

    def query_tile(i, carry):
        qz = {(hh, c): masked_queries(i, hh, c) for hh in heads for c in range(2)}

        def full_tile(t):
            for hh in heads:
                scores(qz[hh, 1], hh, 1, t)
            for hh in heads:
                consume(hh, 0, t)
            for hh in heads:
                scores(qz[hh, 0], hh, 0, t + 1)
            for hh in heads:
                consume(hh, 1, t)

        m_ref[...] = jnp.full(m_ref.shape, -jnp.inf, F32)
        acc_ref[...] = jnp.zeros(acc_ref.shape, F32)

        def eight_tiles(u, c):
            for r in range(8):
                full_tile(8 * u + r)
            return c

        lax.fori_loop(0, jnp.right_shift(i, 3), eight_tiles, 0)

        @pl.when(jnp.bitwise_and(i, 4) == 4)
        def _():
            for r in range(4):
                full_tile(jnp.bitwise_and(i, -8) + r)

        @pl.when(jnp.bitwise_and(i, 2) == 2)
        def _():
            full_tile(jnp.bitwise_and(i, -4))
            full_tile(jnp.bitwise_and(i, -4) + 1)

        @pl.when(jnp.bitwise_and(i, 1) == 1)
        def _():
            full_tile(i - 1)

        for hh in heads:
            consume(hh, 0, i, remask=True)
        for hh in heads:
            consume(hh, 1, i, diag=True)
        nxt = jnp.minimum(i + 1, n_tiles - 1)
        for hh in heads:
            scores(masked_queries(nxt, hh, 0), hh, 0, 0)
        for hh in heads:
            diagonal_scores(nxt, hh)

        rows = pl.ds(pl.multiple_of(i * tq, tq), tq)
        for hh in heads:
            o1 = acc_ref[2 * hh, :V_DIM, :] / acc_ref[2 * hh, V_DIM:V_DIM + 1, :]
            o2 = acc_ref[2 * hh + 1, :V_DIM, :] / acc_ref[2 * hh + 1, V_DIM:V_DIM + 1, :]
            d = (o1 - lam * o2).T
            dn = d * lax.rsqrt(jnp.mean(d * d, axis=-1, keepdims=True) + EPS) * g_ref[...]
            cols = slice(hh * V_DIM, (hh + 1) * V_DIM)
            o_ref[rows, cols] = (dn * (1.0 - lambda_init)
                                 * zs_ref[rows, cols].astype(F32)).astype(BF16)
        return carry

    for hh in heads:
        scores(masked_queries(0, hh, 0), hh, 0, 0)
        diagonal_scores(0, hh)
    lax.fori_loop(0, n_tiles, query_tile, 0)


def _attention(lamv, qt, k, vt, zs, subln_g, lambda_init):
    s = k.shape[0]
    assert Q_TILE == KV_TILE and Q_TILE % CHUNK == 0
    nh = ATTN_HEADS_PER_STEP
    chains = 2 * nh
    return pl.pallas_call(
        functools.partial(_attn_kernel, lambda_init=lambda_init),
        grid=(N_HEADS // nh,),
        in_specs=[
            pl.BlockSpec((4, HEAD_DIM), lambda g: (0, 0)),
            pl.BlockSpec((s // Q_TILE, nh * V_DIM, Q_TILE), lambda g: (0, g, 0)),
            pl.BlockSpec((s, nh * V_DIM), lambda g: (0, g)),
            pl.BlockSpec((nh, s // KV_TILE, V_DIM, KV_TILE), lambda g: (g, 0, 0, 0)),
            pl.BlockSpec((s, nh * V_DIM), lambda g: (0, g)),
            pl.BlockSpec((1, V_DIM), lambda g: (0, 0)),
        ],
        out_specs=pl.BlockSpec((s, nh * V_DIM), lambda g: (0, g)),
        out_shape=jax.ShapeDtypeStruct((s, D_ATTN), BF16),
        scratch_shapes=[
            pltpu.VMEM((chains, KV_TILE, Q_TILE), F32),
            pltpu.VMEM((chains, 1, Q_TILE), F32),
            pltpu.VMEM((nh, KV_TILE, Q_TILE), F32),
            pltpu.VMEM((nh, 1, Q_TILE), F32),
            pltpu.VMEM((chains, 1, Q_TILE), F32),
            pltpu.VMEM((chains, V_DIM + BF16_SUBLANES, Q_TILE), F32),
        ],
        compiler_params=_params(("parallel",), 60),
        name="diff_attn",
    )(lamv, qt, k, vt, zs, subln_g)


def _out_kernel(yc_ref, oz_ref, ga_ref, gb_ref, wc_ref, wa_ref, wo_ref, x_ref, gate_ref,
                *rest, emit_next_h):
    a = _dot(yc_ref[...], wc_ref[...])
    b = _dot(oz_ref[...], wa_ref[...])
    merged = (ga_ref[...].astype(F32) * a + gb_ref[...].astype(F32) * b).astype(BF16)
    x_new = x_ref[...] + gate_ref[...] * _dot(merged, wo_ref[...])
    if emit_next_h:
        g_ref, scale_ref, shift_ref, o_ref, h_ref = rest
        _norm_kernel(x_new, g_ref, scale_ref, shift_ref, h_ref)
    else:
        o_ref, = rest
    o_ref[...] = x_new


def _merge_out(yc, oz, gates, wc, wa, wo, layer, x, gate, next_norm=None):
    s, d = x.shape
    tm = ROW_TILE
    resident = lambda rows: pl.BlockSpec((None, rows, d), lambda m: (layer, 0, 0),
                                         pipeline_mode=pl.Buffered(1))
    row = pl.BlockSpec((1, d), lambda m: (0, 0))
    tile = pl.BlockSpec((tm, d), lambda m: (m, 0))
    emit = next_norm is not None
    return pl.pallas_call(
        functools.partial(_out_kernel, emit_next_h=emit),
        grid=(s // tm,),
        in_specs=[
            pl.BlockSpec((tm, D_CONV), lambda m: (m, 0)),
            pl.BlockSpec((tm, D_ATTN), lambda m: (m, 0)),
            pl.BlockSpec((tm, d), lambda m: (m, 0)),
            pl.BlockSpec((tm, d), lambda m: (m, 1)),
            resident(D_CONV), resident(D_ATTN), resident(d),
            tile, row,
        ] + ([row, row, row] if emit else []),
        out_specs=[tile, tile] if emit else tile,
        out_shape=([jax.ShapeDtypeStruct((s, d), F32), jax.ShapeDtypeStruct((s, d), BF16)]
                   if emit else jax.ShapeDtypeStruct((s, d), F32)),
        compiler_params=_params(("parallel",), 60),
        name="merge_out",
    )(yc, oz, gates, gates, wc, wa, wo, x, gate, *(next_norm if emit else ()))


def kernel(x, c, ada_w, ada_b, norm_g, w_in, conv_w, w_conv_out, q_norm_g, k_norm_g,
           lam_q1, lam_k1, lam_q2, lam_k2, subln_g, w_attn_out, w_o):
    bsz, seq, d = x.shape
    depth = w_in.shape[0]
    assert bsz == 1 and d == D_MODEL and seq % WIDE_ROW_TILE == 0 and seq % Q_TILE == 0
    xs = x[0]
    mod = _modulation(c, ada_w, ada_b)
    cos, sin = _rope_tables(seq)
    wc_b = w_conv_out.astype(BF16)
    wa_b = w_attn_out.astype(BF16)
    wo_b = w_o.astype(BF16)
    lamv = jnp.stack([lam_q1, lam_k1, lam_q2, lam_k2], axis=1).astype(F32)
    mods = [tuple(mod[l, :, i * d:(i + 1) * d] for i in range(3)) for l in range(depth)]
    h = _norm_mod(xs, norm_g[0][None, :], mods[0][1], mods[0][0])
    for l in range(depth):
        gate = mods[l][2]
        yc = _conv_branch(h, w_in, l, conv_w[l])
        qt, k = _qk_branch(h, w_in, l, q_norm_g[l], k_norm_g[l], cos, sin)
        vt = _v_branch(h, w_in, l)
        zs = _act_branch(h, w_in, l, OFF_ZB, D_ATTN, "silu")
        gates = _act_branch(h, w_in, l, OFF_G, 2 * d, "sigmoid")
        lambda_init = 0.8 - 0.6 * math.exp(-0.3 * l)
        oz = _attention(lamv[l], qt, k, vt, zs, subln_g[l][None, :], lambda_init)
        if l + 1 < depth:
            nxt = (norm_g[l + 1][None, :], mods[l + 1][1], mods[l + 1][0])
            xs, h = _merge_out(yc, oz, gates, wc_b, wa_b, wo_b, l, xs, gate, nxt)
        else:
            xs = _merge_out(yc, oz, gates, wc_b, wa_b, wo_b, l, xs, gate)
    return xs[None]
```
